```python
import jax, jax.numpy as jnp
from jax import lax
import numpy as np

D_MODEL = 4096
BATCH = 4
SEQ = 2048
DEPTH = 1
DEC_BATCH = 128
DEC_SEQ = 8
PAST_LEN = 16384
PAGE_SIZE = 128

D_MIX = D_MODEL
D_GLA = D_MIX // 2
D_CONV = D_MIX - D_GLA
GLA_HEADS = 4
GLA_DV = D_GLA // GLA_HEADS
GLA_DK = GLA_DV // 2
GLA_QK = GLA_HEADS * GLA_DK
GATE_RANK = 16
GATE_NORMALIZER = 16.0
GLA_CHUNK = 16
CONV_WIDTH = 31
N_EXPERTS = 32
TOP_K = 4
D_FF = D_MODEL
SWIGLU_LIMIT = 7.0
SWIGLU_ALPHA = 1.702
EXPERT_BLOCK = 128
D_PLE = 256
EPS = 1e-6
SPLITS = (GLA_QK, 2 * GLA_QK, 2 * GLA_QK + D_GLA, 2 * GLA_QK + 2 * D_GLA,
          2 * GLA_QK + 2 * D_GLA + GATE_RANK)
D_IN = 2 * GLA_QK + 2 * D_GLA + GATE_RANK + 2 * D_CONV

kernel_name = "hybrid_gla_conformer_moe_step"


def _rmsnorm(x, g):
    xf = x.astype(jnp.float32)
    y = xf * lax.rsqrt(jnp.mean(xf * xf, axis=-1, keepdims=True) + EPS)
    return y.astype(x.dtype) * g


def _layernorm(x, g, b):
    xf = x.astype(jnp.float32)
    mu = jnp.mean(xf, axis=-1, keepdims=True)
    var = jnp.mean(jnp.square(xf - mu), axis=-1, keepdims=True)
    return ((xf - mu) * lax.rsqrt(var + EPS)).astype(x.dtype) * g + b


def _gla(q, k, v, log_a, s0):
    B, L, H, _ = q.shape
    DV = v.shape[-1]
    pad = (-L) % GLA_CHUNK

    def prep(t):
        t = jnp.pad(t.astype(jnp.float32), ((0, 0), (0, pad), (0, 0), (0, 0)))
        nc = t.shape[1] // GLA_CHUNK
        return jnp.moveaxis(t.reshape(B, nc, GLA_CHUNK, H, t.shape[-1]), 1, 0)

    causal = jnp.tril(jnp.ones((GLA_CHUNK, GLA_CHUNK), dtype=bool))

    def step(S, blk):
        qc, kc, vc, ac = blk
        b = jnp.cumsum(ac, axis=1)
        b_last = b[:, -1]
        qe = qc * jnp.exp(b)
        ke = kc * jnp.exp(-b)
        o_inter = jnp.einsum('bthk,bhkv->bthv', qe, S)
        att = jnp.einsum('bthk,bshk->bhts', qe, ke)
        att = jnp.where(causal[None, None], att, 0.0)
        o_intra = jnp.einsum('bhts,bshv->bthv', att, vc)
        kd = kc * jnp.exp(b_last[:, None] - b)
        S_new = jnp.exp(b_last)[..., None] * S + jnp.einsum('bshk,bshv->bhkv', kd, vc)
        return S_new, o_inter + o_intra

    S, o = lax.scan(step, s0.astype(jnp.float32), (prep(q), prep(k), prep(v), prep(log_a)))
    o = jnp.moveaxis(o, 0, 1).reshape(B, -1, H, DV)[:, :L]
    return o.astype(v.dtype), S.astype(s0.dtype)


def _conformer_conv(u, buf, w, b, ln_g, ln_b):
    up = jnp.concatenate([buf.astype(u.dtype), u], axis=1)
    y = lax.conv_general_dilated(up, w[:, None, :].astype(u.dtype), window_strides=(1,),
                                 padding='VALID', dimension_numbers=('NWC', 'WIO', 'NWC'),
                                 feature_group_count=u.shape[-1]) + b
    y = jax.nn.silu(_layernorm(y, ln_g, ln_b))
    return y, up[:, -(CONV_WIDTH - 1):].astype(buf.dtype)


def _mixer(xn, s_gla, s_conv, w_in, w_gate_lr, b_gate, gla_norm, conv_w, conv_b,
           conv_ln_g, conv_ln_b, w_out):
    B, L, _ = xn.shape
    proj = xn @ w_in
    q, k, v, r, a_lr, glu = jnp.split(proj, SPLITS, axis=-1)
    q = q.reshape(B, L, GLA_HEADS, GLA_DK) * (GLA_DK ** -0.5)
    k = k.reshape(B, L, GLA_HEADS, GLA_DK)
    v = v.reshape(B, L, GLA_HEADS, GLA_DV)
    z = (a_lr @ w_gate_lr + b_gate).astype(jnp.float32)
    log_a = (jax.nn.log_sigmoid(z) / GATE_NORMALIZER).reshape(B, L, GLA_HEADS, GLA_DK)
    o, s_gla_new = _gla(q, k, v, log_a, s_gla)
    o = _rmsnorm(o, gla_norm.reshape(GLA_HEADS, GLA_DV)).reshape(B, L, D_GLA)
    o = o * jax.nn.silu(r)
    ua, ug = jnp.split(glu, 2, axis=-1)
    u = ua * jax.nn.sigmoid(ug)
    c, s_conv_new = _conformer_conv(u, s_conv, conv_w, conv_b, conv_ln_g, conv_ln_b)
    out = jnp.concatenate([o, c], axis=-1) @ w_out
    return out, s_gla_new, s_conv_new


def _moe(xn, w_router, b_router, w_gu, b_gu, w_dn, b_dn):
    shp = xn.shape
    t = xn.reshape(-1, shp[-1])
    T = t.shape[0]
    logits = (t @ w_router + b_router).astype(jnp.float32)
    top_logit, top_idx = lax.top_k(logits, TOP_K)
    gate = jax.nn.softmax(top_logit, axis=-1)
    A = T * TOP_K
    e_flat = top_idx.reshape(-1)
    order = jnp.argsort(e_flat)
    e_sorted = e_flat[order]
    tok_sorted = (order // TOP_K).astype(jnp.int32)
    g_sorted = gate.reshape(-1)[order]
    counts = jnp.bincount(e_flat, length=N_EXPERTS)
    starts = jnp.cumsum(counts) - counts
    padded = (counts + EXPERT_BLOCK - 1) // EXPERT_BLOCK * EXPERT_BLOCK
    pad_ends = jnp.cumsum(padded)
    pad_starts = pad_ends - padded
    dest = pad_starts[e_sorted] + jnp.arange(A) - starts[e_sorted]
    n_blocks = -(-A // EXPERT_BLOCK) + N_EXPERTS
    R = n_blocks * EXPERT_BLOCK
    tok_buf = jnp.full((R,), T, jnp.int32).at[dest].set(tok_sorted)
    gate_buf = jnp.zeros((R,), jnp.float32).at[dest].set(g_sorted)
    block_expert = jnp.minimum(
        jnp.searchsorted(pad_ends, jnp.arange(n_blocks) * EXPERT_BLOCK, side='right'),
        N_EXPERTS - 1)
    t_ext = jnp.concatenate([t, jnp.zeros((1, t.shape[-1]), t.dtype)], axis=0)
    xb = t_ext[tok_buf].reshape(n_blocks, EXPERT_BLOCK, t.shape[-1])

    def expert_block(args):
        xblk, e = args
        h = xblk @ w_gu[e] + b_gu[e]
        g, up = jnp.split(h, 2, axis=-1)
        g = jnp.minimum(g, SWIGLU_LIMIT)
        up = jnp.clip(up, -SWIGLU_LIMIT, SWIGLU_LIMIT)
        act = g * jax.nn.sigmoid(SWIGLU_ALPHA * g) * (up + 1.0)
        return act @ w_dn[e] + b_dn[e]

    yb = lax.map(expert_block, (xb, block_expert)).reshape(R, t.shape[-1])
    yb = yb * gate_buf[:, None].astype(yb.dtype)
    y = jax.ops.segment_sum(yb, tok_buf, num_segments=T + 1)[:T]
    return y.reshape(shp)


def _trunk(x, p, s_gla, s_conv, ln_mix, w_in, w_gate_lr, b_gate, gla_norm, conv_w, conv_b,
           conv_ln_g, conv_ln_b, w_out, ln_moe, w_router, b_router, w_gu, b_gu, w_dn, b_dn,
           ln_ple, w_ple_gate, w_ple_proj, ln_final):
    h = x
    new_gla, new_conv = [], []
    for i in range(DEPTH):
        mix, sg, sc = _mixer(_rmsnorm(h, ln_mix[i]), s_gla[i], s_conv[i], w_in[i], w_gate_lr[i],
                             b_gate[i], gla_norm[i], conv_w[i], conv_b[i], conv_ln_g[i],
                             conv_ln_b[i], w_out[i])
        h = h + mix
        h = h + _moe(_rmsnorm(h, ln_moe[i]), w_router[i], b_router[i], w_gu[i], b_gu[i],
                     w_dn[i], b_dn[i])
        gate = jax.nn.sigmoid(_rmsnorm(h, ln_ple[i]) @ w_ple_gate[i])
        h = h + gate * (p[i] @ w_ple_proj[i])
        new_gla.append(sg)
        new_conv.append(sc)
    return _rmsnorm(h, ln_final), jnp.stack(new_gla, 0), jnp.stack(new_conv, 0)


def setup_inputs(seed: int = 0) -> dict:
    key = jax.random.key(seed)
    ks = jax.random.split(key, 32)
    f32 = jnp.float32
    nrm = lambda k, s, sc: jax.random.normal(k, s, f32) * sc
    gain = lambda k, s: 1.0 + 0.1 * jax.random.normal(k, s, f32)
    return {
        "x_prompt": nrm(ks[0], (BATCH, SEQ, D_MODEL), 1.0),
        "x_sample": nrm(ks[1], (DEC_BATCH, DEC_SEQ, D_MODEL), 1.0),
        "p_prompt": nrm(ks[2], (DEPTH, BATCH, SEQ, D_PLE), 1.0),
        "p_sample": nrm(ks[3], (DEPTH, DEC_BATCH, DEC_SEQ, D_PLE), 1.0),
        "state_gla": nrm(ks[4], (DEPTH, DEC_BATCH, GLA_HEADS, GLA_DK, GLA_DV), 1.0),
        "state_conv": nrm(ks[5], (DEPTH, DEC_BATCH, CONV_WIDTH - 1, D_CONV), 0.5),
        "ln_mix": gain(ks[6], (DEPTH, D_MODEL)),
        "w_in": nrm(ks[7], (DEPTH, D_MODEL, D_IN), D_MODEL ** -0.5),
        "w_gate_lr": nrm(ks[8], (DEPTH, GATE_RANK, GLA_QK), GATE_RANK ** -0.5),
        "b_gate": nrm(ks[9], (DEPTH, GLA_QK), 0.1),
        "gla_norm": gain(ks[10], (DEPTH, D_GLA)),
        "conv_w": nrm(ks[11], (DEPTH, CONV_WIDTH, D_CONV), CONV_WIDTH ** -0.5),
        "conv_b": nrm(ks[12], (DEPTH, D_CONV), 0.02),
        "conv_ln_g": gain(ks[13], (DEPTH, D_CONV)),
        "conv_ln_b": nrm(ks[14], (DEPTH, D_CONV), 0.02),
        "w_out": nrm(ks[15], (DEPTH, D_MIX, D_MODEL), D_MIX ** -0.5),
        "ln_moe": gain(ks[16], (DEPTH, D_MODEL)),
        "w_router": nrm(ks[17], (DEPTH, D_MODEL, N_EXPERTS), D_MODEL ** -0.5),
        "b_router": nrm(ks[18], (DEPTH, N_EXPERTS), 0.01),
        "w_gu": nrm(ks[19], (DEPTH, N_EXPERTS, D_MODEL, 2 * D_FF), D_MODEL ** -0.5),
        "b_gu": nrm(ks[20], (DEPTH, N_EXPERTS, 2 * D_FF), 0.01),
        "w_dn": nrm(ks[21], (DEPTH, N_EXPERTS, D_FF, D_MODEL), D_FF ** -0.5),
        "b_dn": nrm(ks[22], (DEPTH, N_EXPERTS, D_MODEL), 0.01),
        "ln_ple": gain(ks[23], (DEPTH, D_MODEL)),
        "w_ple_gate": nrm(ks[24], (DEPTH, D_MODEL, D_MODEL), D_MODEL ** -0.5),
        "w_ple_proj": nrm(ks[25], (DEPTH, D_PLE, D_MODEL), D_PLE ** -0.5),
        "ln_final": gain(ks[26], (D_MODEL,)),
    }


def reference(x_prompt, x_sample, p_prompt, p_sample, state_gla, state_conv, ln_mix, w_in,
              w_gate_lr, b_gate, gla_norm, conv_w, conv_b, conv_ln_g, conv_ln_b, w_out, ln_moe,
              w_router, b_router, w_gu, b_gu, w_dn, b_dn, ln_ple, w_ple_gate, w_ple_proj,
              ln_final):
    zero_gla = jnp.zeros((DEPTH, BATCH, GLA_HEADS, GLA_DK, GLA_DV), state_gla.dtype)
    zero_conv = jnp.zeros((DEPTH, BATCH, CONV_WIDTH - 1, D_CONV), state_conv.dtype)
    y_prompt, gla_prompt, conv_prompt = _trunk(
        x_prompt, p_prompt, zero_gla, zero_conv, ln_mix, w_in, w_gate_lr, b_gate, gla_norm,
        conv_w, conv_b, conv_ln_g, conv_ln_b, w_out, ln_moe, w_router, b_router, w_gu, b_gu,
        w_dn, b_dn, ln_ple, w_ple_gate, w_ple_proj, ln_final)
    y_sample, gla_sample, conv_sample = _trunk(
        x_sample, p_sample, state_gla, state_conv, ln_mix, w_in, w_gate_lr, b_gate, gla_norm,
        conv_w, conv_b, conv_ln_g, conv_ln_b, w_out, ln_moe, w_router, b_router, w_gu, b_gu,
        w_dn, b_dn, ln_ple, w_ple_gate, w_ple_proj, ln_final)
    return (y_prompt, y_sample, gla_prompt, conv_prompt, gla_sample, conv_sample)
```

```python
import functools

import jax
import jax.numpy as jnp
from jax import lax
from jax.experimental import pallas as pl
from jax.experimental.pallas import tpu as pltpu

EPS = 1e-6
TOP_K = 4
GATE_NORMALIZER = 16.0
GLA_CHUNK = 16
SWIGLU_LIMIT = 7.0
SWIGLU_ALPHA = 1.702

LANES = 128
VMEM_LIMIT_BYTES = 58 * 2**20
VMEM_BUDGET_BYTES = 48 * 2**20
RANK_BITS = 16

F32 = jnp.float32
BF16 = jnp.bfloat16


def _tile(n, pref, align=8):
    if n <= pref:
        return n
    for t in range(pref, 0, -1):
        if n % t == 0 and t % align == 0:
            return t
    return n


def _params(sem):
    return pltpu.CompilerParams(dimension_semantics=sem, vmem_limit_bytes=VMEM_LIMIT_BYTES)


def _rms_body(x_ref, g_ref, o_ref):
    x = x_ref[...]
    y = x * lax.rsqrt(jnp.mean(x * x, axis=-1, keepdims=True) + EPS)
    o_ref[...] = (y * g_ref[...]).astype(o_ref.dtype)


def _rmsnorm(x, g, out_dtype, row0=0, rows=None):
    rows = x.shape[0] if rows is None else rows
    d = x.shape[1]
    tm = _tile(rows, 512)
    assert row0 % tm == 0
    off = row0 // tm
    return pl.pallas_call(
        _rms_body,
        grid=(rows // tm,),
        in_specs=[pl.BlockSpec((tm, d), lambda i: (i + off, 0)),
                  pl.BlockSpec((1, d), lambda i: (0, 0))],
        out_specs=pl.BlockSpec((tm, d), lambda i: (i, 0)),
        out_shape=jax.ShapeDtypeStruct((rows, d), out_dtype),
        compiler_params=_params(("arbitrary",)),
        name="rmsnorm",
    )(x, g.reshape(1, d))


def _gmm_body(be_ref, nu_ref, *refs, n_w, n_b, n_extra, epilogue, cast_rows):
    x_ref = refs[0]
    w_refs = refs[1:1 + n_w]
    b_refs = refs[1 + n_w:1 + n_w + n_b]
    e_refs = refs[1 + n_w + n_b:1 + n_w + n_b + n_extra]
    o_ref = refs[1 + n_w + n_b + n_extra]
    wb_refs = refs[2 + n_w + n_b + n_extra:]
    i = pl.program_id(1)
    valid = i < nu_ref[0]
    prev = jnp.maximum(i - 1, 0)
    changed = jnp.logical_or(i == 0, be_ref[i] != be_ref[prev])
    k = w_refs[0].shape[0]

    @pl.when(jnp.logical_and(valid, changed))
    def _():
        def cast(r, carry):
            sl = pl.ds(pl.multiple_of(r * cast_rows, cast_rows), cast_rows)
            for w_ref, wb_ref in zip(w_refs, wb_refs):
                wb_ref[sl, :] = w_ref[sl, :].astype(BF16)
            return carry
        lax.fori_loop(0, k // cast_rows, cast, 0)

    @pl.when(valid)
    def _():
        x = x_ref[...]
        accs = [jnp.dot(x, wb_ref[...], preferred_element_type=F32) for wb_ref in wb_refs]
        o_ref[...] = epilogue(accs, [b[...] for b in b_refs], [e[...] for e in e_refs]).astype(o_ref.dtype)

    @pl.when(jnp.logical_not(valid))
    def _():
        o_ref[...] = jnp.zeros(o_ref.shape, o_ref.dtype)


def _gmm(x, w, col_offsets, bias, be, nu, *, tm, tn, n_out, out_dtype, epilogue, extras=(), name):
    r, k = x.shape
    nb = r // tm
    assert r % tm == 0 and n_out % tn == 0 and all(o % tn == 0 for o in col_offsets)
    n_w = len(col_offsets)

    def row(i, nu_ref):
        return jnp.minimum(i, nu_ref[0] - 1)

    in_specs = [pl.BlockSpec((tm, k), lambda j, i, be_r, nu_r: (row(i, nu_r), 0))]
    args = [x]
    for off in col_offsets:
        ob = off // tn
        in_specs.append(pl.BlockSpec((None, k, tn),
                                     lambda j, i, be_r, nu_r, ob=ob: (be_r[row(i, nu_r)], 0, j + ob)))
        args.append(w)
    n_b = 0
    if bias is not None:
        for off in col_offsets:
            ob = off // tn
            in_specs.append(pl.BlockSpec((None, 1, tn),
                                         lambda j, i, be_r, nu_r, ob=ob: (be_r[row(i, nu_r)], 0, j + ob)))
            args.append(bias)
            n_b += 1
    for arr, kind in extras:
        if kind == "tile":
            in_specs.append(pl.BlockSpec((tm, tn), lambda j, i, be_r, nu_r: (row(i, nu_r), j)))
        elif kind == "row":
            in_specs.append(pl.BlockSpec((tm, arr.shape[1]), lambda j, i, be_r, nu_r: (row(i, nu_r), 0)))
        else:
            in_specs.append(pl.BlockSpec((arr.shape[0], tn), lambda j, i, be_r, nu_r: (0, j)))
        args.append(arr)
    vmem = (2 * tm * k * 2 + n_w * (2 * k * tn * 4 + k * tn * 2) + n_w * tm * tn * 4
            + 2 * tm * tn * jnp.dtype(out_dtype).itemsize)
    for arr, kind in extras:
        blk = {"tile": tm * tn, "row": tm * arr.shape[1], "col": arr.shape[0] * tn}[kind]
        vmem += 2 * blk * arr.dtype.itemsize
    assert vmem <= VMEM_BUDGET_BYTES, (name, vmem)
    cast_rows = _tile(k, 256)
    body = functools.partial(_gmm_body, n_w=n_w, n_b=n_b, n_extra=len(extras), epilogue=epilogue,
                             cast_rows=cast_rows)
    return pl.pallas_call(
        body,
        grid_spec=pltpu.PrefetchScalarGridSpec(
            num_scalar_prefetch=2,
            grid=(n_out // tn, nb),
            in_specs=in_specs,
            out_specs=pl.BlockSpec((tm, tn), lambda j, i, be_r, nu_r: (i, j)),
            scratch_shapes=[pltpu.VMEM((k, tn), BF16) for _ in range(n_w)]),
        out_shape=jax.ShapeDtypeStruct((r, n_out), out_dtype),
        compiler_params=_params(("arbitrary", "arbitrary")),
        name=name,
    )(be, nu, *args)


def _dense(x, w2d, col_offsets, *, tm, tn, n_out, out_dtype, epilogue, extras=(), name):
    nb = x.shape[0] // tm
    return _gmm(x, w2d[None], col_offsets, None, jnp.zeros((nb,), jnp.int32),
                jnp.full((1,), nb, jnp.int32), tm=tm, tn=tn, n_out=n_out, out_dtype=out_dtype,
                epilogue=epilogue, extras=extras, name=name)


def _sigmoid(x):
    return 1.0 / (1.0 + jnp.exp(-x))


def _ep_plain(accs, biases, extras):
    return accs[0]


def _ep_glu(accs, biases, extras):
    return accs[0] * _sigmoid(accs[1])


def _ep_residual(accs, biases, extras):
    return accs[0] + extras[0]


def _ep_swiglu(accs, biases, extras):
    g = jnp.minimum(accs[0] + biases[0], SWIGLU_LIMIT)
    up = jnp.clip(accs[1] + biases[1], -SWIGLU_LIMIT, SWIGLU_LIMIT)
    return g * _sigmoid(SWIGLU_ALPHA * g) * (up + 1.0)


def _ep_bias(accs, biases, extras):
    return accs[0] + biases[0]


def _ep_ple(accs, biases, extras):
    h, p, wp = extras
    pp = jnp.dot(p, wp.astype(BF16), preferred_element_type=F32)
    return h + _sigmoid(accs[0]) * pp


def _loga_body(x_ref, wa_ref, wg_ref, bg_ref, o_ref):
    a = jnp.dot(x_ref[...], wa_ref[...], preferred_element_type=F32)
    z = jnp.dot(a.astype(BF16), wg_ref[...], preferred_element_type=F32) + bg_ref[...]
    log_sig = jnp.minimum(z, 0.0) - jnp.log1p(jnp.exp(-jnp.abs(z)))
    o_ref[...] = log_sig / GATE_NORMALIZER


def _log_decay(xn, w_alr, w_gate_lr, b_gate):
    t, d = xn.shape
    rank, qk = w_gate_lr.shape
    assert rank <= LANES
    wa = jnp.zeros((d, LANES), BF16).at[:, :rank].set(w_alr.astype(BF16))
    wg = jnp.zeros((LANES, qk), BF16).at[:rank, :].set(w_gate_lr.astype(BF16))
    tm = _tile(t, 512)
    return pl.pallas_call(
        _loga_body,
        grid=(t // tm,),
        in_specs=[pl.BlockSpec((tm, d), lambda i: (i, 0)),
                  pl.BlockSpec((d, LANES), lambda i: (0, 0)),
                  pl.BlockSpec((LANES, qk), lambda i: (0, 0)),
                  pl.BlockSpec((1, qk), lambda i: (0, 0))],
        out_specs=pl.BlockSpec((tm, qk), lambda i: (i, 0)),
        out_shape=jax.ShapeDtypeStruct((t, qk), F32),
        compiler_params=_params(("arbitrary",)),
        name="gla_log_decay",
    )(xn, wa, wg, b_gate.reshape(1, qk))


def _split3(x):
    hi = x.astype(BF16)
    r1 = x - hi.astype(F32)
    mid = r1.astype(BF16)
    lo = (r1 - mid.astype(F32)).astype(BF16)
    return hi, mid, lo


def _gla_body(q_ref, k_ref, v_ref, r_ref, la_ref, s0_ref, gn_ref, og_ref, sn_ref, s_ref,
              *, chunk, n_sub, scale):
    c = pl.program_id(2)

    @pl.when(c == 0)
    def _():
        s_ref[...] = s0_ref[...]

    dk, dv = s_ref.shape
    rows = lax.broadcasted_iota(jnp.int32, (chunk, chunk), 0)
    cols = lax.broadcasted_iota(jnp.int32, (chunk, chunk), 1)
    causal = rows >= cols
    tril = causal.astype(BF16)
    ones = jnp.ones((chunk, LANES), BF16)
    tdims = (((0,), (0,)), ((), ()))

    def sub(i, carry):
        sl = pl.ds(pl.multiple_of(i * chunk, chunk), chunk)
        q = q_ref[sl, :] * scale
        k = k_ref[sl, :]
        v = v_ref[sl, :].astype(BF16)
        la = la_ref[sl, :]
        pieces = _split3(la)
        b = sum(jnp.dot(tril, p, preferred_element_type=F32) for p in pieces)
        b_last = b[chunk - 1:chunk, :]
        b_last_col = sum(lax.dot_general(p, ones, tdims, preferred_element_type=F32) for p in pieces)
        decay = jnp.exp(b_last_col[:, :1])
        qe = (q * jnp.exp(b)).astype(BF16)
        ke = (k * jnp.exp(-b)).astype(BF16)
        kd = (k * jnp.exp(b_last - b)).astype(BF16)
        s = s_ref[...]
        o = jnp.dot(qe, s.astype(BF16), preferred_element_type=F32)
        att = lax.dot_general(qe, ke, (((1,), (1,)), ((), ())), preferred_element_type=F32)
        att = jnp.where(causal, att, 0.0).astype(BF16)
        o = o + jnp.dot(att, v, preferred_element_type=F32)
        s_ref[...] = decay * s + lax.dot_general(kd, v, tdims, preferred_element_type=F32)
        o = o * lax.rsqrt(jnp.mean(o * o, axis=-1, keepdims=True) + EPS) * gn_ref[...]
        r = r_ref[sl, :]
        og_ref[sl, :] = (o * (r * _sigmoid(r))).astype(og_ref.dtype)
        return carry

    lax.fori_loop(0, n_sub, sub, 0)

    @pl.when(c == pl.num_programs(2) - 1)
    def _():
        sn_ref[...] = s_ref[...]


def _gla(qkvr, log_a, s0, gla_norm, *, row0, batch, length):
    _, heads, dk, dv = s0.shape
    chunk = min(GLA_CHUNK, length)
    lc = _tile(length, 256, chunk)
    nc = length // lc
    assert length % chunk == 0 and row0 % lc == 0 and (heads * dk) % dv == 0
    rb0 = row0 // lc
    v_off = 2 * heads * dk // dv
    r_off = v_off + heads
    body = functools.partial(_gla_body, chunk=chunk, n_sub=lc // chunk, scale=dk ** -0.5)
    rowblk = lambda b, c: rb0 + b * nc + c
    og, s_new = pl.pallas_call(
        body,
        grid=(batch, heads, nc),
        in_specs=[pl.BlockSpec((lc, dk), lambda b, h, c: (rowblk(b, c), h)),
                  pl.BlockSpec((lc, dk), lambda b, h, c: (rowblk(b, c), heads + h)),
                  pl.BlockSpec((lc, dv), lambda b, h, c: (rowblk(b, c), v_off + h)),
                  pl.BlockSpec((lc, dv), lambda b, h, c: (rowblk(b, c), r_off + h)),
                  pl.BlockSpec((lc, dk), lambda b, h, c: (rowblk(b, c), h)),
                  pl.BlockSpec((None, None, dk, dv), lambda b, h, c: (b, h, 0, 0)),
                  pl.BlockSpec((1, dv), lambda b, h, c: (0, h))],
        out_specs=[pl.BlockSpec((lc, dv), lambda b, h, c: (b * nc + c, h)),
                   pl.BlockSpec((None, None, dk, dv), lambda b, h, c: (b, h, 0, 0))],
        out_shape=[jax.ShapeDtypeStruct((batch * length, heads * dv), F32),
                   jax.ShapeDtypeStruct((batch, heads, dk, dv), F32)],
        scratch_shapes=[pltpu.VMEM((dk, dv), F32)],
        compiler_params=_params(("arbitrary", "arbitrary", "arbitrary")),
        name="gla",
    )(qkvr, qkvr, qkvr, qkvr, log_a, s0, gla_norm.reshape(1, heads * dv))
    return og, s_new


def _conv_body(u_ref, st_ref, w_ref, b_ref, g_ref, beta_ref, c_ref, sn_ref, ext_ref, y_ref,
               *, width, lc, pad, row_tile, ch_tile):
    c = pl.program_id(1)
    hist = width - 1
    ch = u_ref.shape[1]

    @pl.when(c == 0)
    def _():
        if pad:
            ext_ref[0:pad, :] = jnp.zeros((pad, ch), F32)
        ext_ref[pad:pad + hist, :] = st_ref[...]

    base = pad + hist
    ext_ref[base:base + lc, :] = u_ref[...]

    def taps(ci, carry):
        cs = pl.ds(pl.multiple_of(ci * ch_tile, ch_tile), ch_tile)
        for r0 in range(0, lc, row_tile):
            acc = jnp.zeros((row_tile, ch_tile), F32) + b_ref[:, cs]
            for j in range(width):
                acc = acc + w_ref[j:j + 1, cs] * ext_ref[pad + j + r0:pad + j + r0 + row_tile, cs]
            y_ref[r0:r0 + row_tile, cs] = acc
        return carry
    lax.fori_loop(0, ch // ch_tile, taps, 0)

    acc = y_ref[...]
    mu = jnp.mean(acc, axis=-1, keepdims=True)
    cen = acc - mu
    var = jnp.mean(cen * cen, axis=-1, keepdims=True)
    y = cen * lax.rsqrt(var + EPS) * g_ref[...] + beta_ref[...]
    c_ref[...] = (y * _sigmoid(y)).astype(c_ref.dtype)
    tail = ext_ref[pad + lc:pad + lc + hist, :]
    ext_ref[pad:pad + hist, :] = tail

    @pl.when(c == pl.num_programs(1) - 1)
    def _():
        sn_ref[...] = tail


def _conv(u, state, conv_w, conv_b, ln_g, ln_b, *, row0, batch, length):
    ch = u.shape[1]
    width = conv_w.shape[0]
    hist = width - 1
    lc = _tile(length, 128)
    nc = length // lc
    assert row0 % lc == 0 and lc % 8 == 0
    rb0 = row0 // lc
    pad = (-hist) % 8
    row_tile = _tile(lc, 32)
    body = functools.partial(_conv_body, width=width, lc=lc, pad=pad, row_tile=row_tile,
                             ch_tile=_tile(ch, 2 * LANES, LANES))
    vec = lambda a: a.reshape(1, ch)
    return pl.pallas_call(
        body,
        grid=(batch, nc),
        in_specs=[pl.BlockSpec((lc, ch), lambda b, c: (rb0 + b * nc + c, 0)),
                  pl.BlockSpec((None, hist, ch), lambda b, c: (b, 0, 0)),
                  pl.BlockSpec((width, ch), lambda b, c: (0, 0)),
                  pl.BlockSpec((1, ch), lambda b, c: (0, 0)),
                  pl.BlockSpec((1, ch), lambda b, c: (0, 0)),
                  pl.BlockSpec((1, ch), lambda b, c: (0, 0))],
        out_specs=[pl.BlockSpec((lc, ch), lambda b, c: (b * nc + c, 0)),
                   pl.BlockSpec((None, hist, ch), lambda b, c: (b, 0, 0))],
        out_shape=[jax.ShapeDtypeStruct((batch * length, ch), F32),
                   jax.ShapeDtypeStruct((batch, hist, ch), F32)],
        scratch_shapes=[pltpu.VMEM((pad + hist + lc, ch), F32), pltpu.VMEM((lc, ch), F32)],
        compiler_params=_params(("arbitrary", "arbitrary")),
        name="conformer_conv",
    )(u, state, conv_w, vec(conv_b), vec(ln_g), vec(ln_b))


def _router_body(h_ref, g_ref, w_ref, b_ref, xn_ref, code_ref, gate_ref, cnt_ref, carry_ref,
                 *, n_experts):
    i = pl.program_id(0)

    @pl.when(i == 0)
    def _():
        carry_ref[...] = jnp.zeros(carry_ref.shape, F32)

    h = h_ref[...]
    xn = h * lax.rsqrt(jnp.mean(h * h, axis=-1, keepdims=True) + EPS) * g_ref[...]
    xn_ref[...] = xn
    tm = h.shape[0]
    x_hi = xn.astype(BF16)
    x_lo = (xn - x_hi.astype(F32)).astype(BF16)
    w = w_ref[...]
    w_hi = w.astype(BF16)
    w_lo = (w - w_hi.astype(F32)).astype(BF16)
    logits = (jnp.dot(x_hi, w_hi, preferred_element_type=F32)
              + jnp.dot(x_lo, w_hi, preferred_element_type=F32)
              + jnp.dot(x_hi, w_lo, preferred_element_type=F32)) + b_ref[...]
    lane = lax.broadcasted_iota(jnp.int32, (tm, LANES), 1)
    lane_f = lane.astype(F32)
    work = jnp.where(lane < n_experts, logits, -jnp.inf)
    tops, hots = [], []
    for _ in range(TOP_K):
        m = jnp.max(work, axis=-1, keepdims=True)
        idx = jnp.min(jnp.where(work == m, lane_f, float(LANES)), axis=-1, keepdims=True)
        hot = lane_f == idx
        tops.append(m)
        hots.append(hot)
        work = jnp.where(hot, -jnp.inf, work)
    exps = [jnp.exp(t - tops[0]) for t in tops]
    denom = sum(exps)
    chosen = sum(hot.astype(F32) for hot in hots)
    rows = lax.broadcasted_iota(jnp.int32, (tm, tm), 0)
    cols = lax.broadcasted_iota(jnp.int32, (tm, tm), 1)
    before = (rows > cols).astype(BF16)
    seen = jnp.dot(before, chosen.astype(BF16), preferred_element_type=F32) + carry_ref[0:1, :]
    code = jnp.zeros((tm, LANES), jnp.int32)
    gate = jnp.zeros((tm, LANES), F32)
    for kk in range(TOP_K):
        hot_f = hots[kk].astype(F32)
        rank = jnp.sum(seen * hot_f, axis=-1, keepdims=True)
        expert = jnp.sum(lane_f * hot_f, axis=-1, keepdims=True)
        packed = expert.astype(jnp.int32) * (1 << RANK_BITS) + rank.astype(jnp.int32)
        code = jnp.where(lane == kk, packed, code)
        gate = jnp.where(lane == kk, exps[kk] / denom, gate)
    code_ref[...] = code
    gate_ref[...] = gate
    total = carry_ref[...] + jnp.sum(chosen, axis=0, keepdims=True)
    carry_ref[...] = total
    cnt_ref[...] = total


def _router(h, ln_g, w_router, b_router):
    t, d = h.shape
    n_experts = w_router.shape[1]
    assert n_experts <= LANES and t < (1 << RANK_BITS)
    wr = jnp.zeros((d, LANES), F32).at[:, :n_experts].set(w_router)
    br = jnp.zeros((1, LANES), F32).at[0, :n_experts].set(b_router)
    tm = _tile(t, 256)
    body = functools.partial(_router_body, n_experts=n_experts)
    return pl.pallas_call(
        body,
        grid=(t // tm,),
        in_specs=[pl.BlockSpec((tm, d), lambda i: (i, 0)),
                  pl.BlockSpec((1, d), lambda i: (0, 0)),
                  pl.BlockSpec((d, LANES), lambda i: (0, 0)),
                  pl.BlockSpec((1, LANES), lambda i: (0, 0))],
        out_specs=[pl.BlockSpec((tm, d), lambda i: (i, 0)),
                   pl.BlockSpec((tm, LANES), lambda i: (i, 0)),
                   pl.BlockSpec((tm, LANES), lambda i: (i, 0)),
                   pl.BlockSpec((8, LANES), lambda i: (0, 0))],
        out_shape=[jax.ShapeDtypeStruct((t, d), F32),
                   jax.ShapeDtypeStruct((t, LANES), jnp.int32),
                   jax.ShapeDtypeStruct((t, LANES), F32),
                   jax.ShapeDtypeStruct((8, LANES), F32)],
        scratch_shapes=[pltpu.VMEM((8, LANES), F32)],
        compiler_params=_params(("arbitrary",)),
        name="moe_router",
    )(h, ln_g.reshape(1, d), wr, br)


def _slot(code_ref, ps_ref, a):
    p = code_ref[a]
    return ps_ref[p >> RANK_BITS] + (p & ((1 << RANK_BITS) - 1))


def _dispatch_body(code_ref, ps_ref, nu_ref, x_hbm, o_ref, tok_ref, buf_ref, sem, *, n_assign, tm):
    i = pl.program_id(0)

    @pl.when(i == 0)
    def _():
        def clear(s):
            tok_ref[s] = 0
            return s + 1
        lax.while_loop(lambda s: s < nu_ref[0] * tm, clear, 0)

        def place(a):
            tok_ref[_slot(code_ref, ps_ref, a)] = a // TOP_K
            return a + 1
        lax.while_loop(lambda a: a < n_assign, place, 0)

    @pl.when(i < nu_ref[0])
    def _():
        def row_copy(r):
            return pltpu.make_async_copy(x_hbm.at[pl.ds(tok_ref[i * tm + r], 1), :],
                                         buf_ref.at[pl.ds(r, 1), :], sem)

        def start(r, carry):
            row_copy(r).start()
            return carry
        lax.fori_loop(0, tm, start, 0)

        def wait(r, carry):
            row_copy(r).wait()
            return carry
        lax.fori_loop(0, tm, wait, 0)
        o_ref[...] = buf_ref[...].astype(o_ref.dtype)

    @pl.when(i >= nu_ref[0])
    def _():
        o_ref[...] = jnp.zeros(o_ref.shape, o_ref.dtype)


def _dispatch(xn, code, pad_starts, nu, *, n_blocks, tm):
    t, d = xn.shape
    n_assign = code.shape[0]
    body = functools.partial(_dispatch_body, n_assign=n_assign, tm=tm)
    return pl.pallas_call(
        body,
        grid_spec=pltpu.PrefetchScalarGridSpec(
            num_scalar_prefetch=3,
            grid=(n_blocks,),
            in_specs=[pl.BlockSpec(memory_space=pl.ANY)],
            out_specs=pl.BlockSpec((tm, d), lambda i, c_r, p_r, nu_r: (i, 0)),
            scratch_shapes=[pltpu.SMEM((n_blocks * tm,), jnp.int32),
                            pltpu.VMEM((tm, d), F32),
                            pltpu.SemaphoreType.DMA(())]),
        out_shape=jax.ShapeDtypeStruct((n_blocks * tm, d), BF16),
        compiler_params=_params(("arbitrary",)),
        name="moe_dispatch",
    )(code, pad_starts, nu, xn)


def _combine_body(code_ref, ps_ref, y_hbm, h_ref, gate_ref, g_ref, h_out_ref, xn_ref, buf_ref, sem, *, tm):
    i = pl.program_id(0)

    def row_copy(r, kk):
        slot = _slot(code_ref, ps_ref, (i * tm + r) * TOP_K + kk)
        return pltpu.make_async_copy(y_hbm.at[pl.ds(slot, 1), :], buf_ref.at[kk, pl.ds(r, 1), :], sem)

    def start(r, carry):
        for kk in range(TOP_K):
            row_copy(r, kk).start()
        return carry
    lax.fori_loop(0, tm, start, 0)

    def wait(r, carry):
        for kk in range(TOP_K):
            row_copy(r, kk).wait()
        return carry
    lax.fori_loop(0, tm, wait, 0)

    gate = gate_ref[...]
    h = h_ref[...]
    for kk in range(TOP_K):
        h = h + gate[:, kk:kk + 1] * buf_ref[kk]
    h_out_ref[...] = h
    xn = h * lax.rsqrt(jnp.mean(h * h, axis=-1, keepdims=True) + EPS) * g_ref[...]
    xn_ref[...] = xn.astype(xn_ref.dtype)


def _combine(yb, h, gate, code, pad_starts, ln_g):
    t, d = h.shape
    tm = _tile(t, 128)
    body = functools.partial(_combine_body, tm=tm)
    return pl.pallas_call(
        body,
        grid_spec=pltpu.PrefetchScalarGridSpec(
            num_scalar_prefetch=2,
            grid=(t // tm,),
            in_specs=[pl.BlockSpec(memory_space=pl.ANY),
                      pl.BlockSpec((tm, d), lambda i, c_r, p_r: (i, 0)),
                      pl.BlockSpec((tm, LANES), lambda i, c_r, p_r: (i, 0)),
                      pl.BlockSpec((1, d), lambda i, c_r, p_r: (0, 0))],
            out_specs=[pl.BlockSpec((tm, d), lambda i, c_r, p_r: (i, 0)),
                       pl.BlockSpec((tm, d), lambda i, c_r, p_r: (i, 0))],
            scratch_shapes=[pltpu.VMEM((TOP_K, tm, d), F32),
                            pltpu.SemaphoreType.DMA(())]),
        out_shape=[jax.ShapeDtypeStruct((t, d), F32),
                   jax.ShapeDtypeStruct((t, d), BF16)],
        compiler_params=_params(("arbitrary",)),
        name="moe_combine",
    )(code, pad_starts, yb, h, gate, ln_g.reshape(1, d))


def _layer(x, p, groups, s_gla, s_conv, ln_mix, w_in, w_gate_lr, b_gate, gla_norm, conv_w, conv_b,
           conv_ln_g, conv_ln_b, w_out, ln_moe, w_router, b_router, w_gu, b_gu, w_dn, b_dn,
           ln_ple, w_ple_gate, w_ple_proj):
    t, d = x.shape
    heads, dk, dv = s_gla[0].shape[1:]
    qk, d_gla = heads * dk, heads * dv
    rank = w_gate_lr.shape[0]
    d_conv = conv_w.shape[1]
    n_experts, _, d_ff2 = w_gu.shape
    d_ff = d_ff2 // 2
    n_qkvr = 2 * qk + 2 * d_gla
    glu0 = n_qkvr + rank
    tm_dense = _tile(t, 1024)
    tn = 512

    xn1 = _rmsnorm(x, ln_mix, BF16)
    qkvr = _dense(xn1, w_in, [0], tm=tm_dense, tn=_tile(n_qkvr, tn, LANES), n_out=n_qkvr, out_dtype=F32,
                  epilogue=_ep_plain, name="in_proj_qkvr")
    u = _dense(xn1, w_in[:, glu0:], [0, d_conv], tm=tm_dense, tn=_tile(d_conv, tn // 2, LANES), n_out=d_conv,
               out_dtype=F32, epilogue=_ep_glu, name="in_proj_glu")
    log_a = _log_decay(xn1, w_in[:, n_qkvr:glu0], w_gate_lr, b_gate)
    og, cv, new_gla, new_conv = [], [], [], []
    for (row0, batch, length), sg, sc in zip(groups, s_gla, s_conv):
        o_g, s_g = _gla(qkvr, log_a, sg, gla_norm, row0=row0, batch=batch, length=length)
        o_c, s_c = _conv(u, sc, conv_w, conv_b, conv_ln_g, conv_ln_b, row0=row0, batch=batch, length=length)
        og.append(o_g)
        cv.append(o_c)
        new_gla.append(s_g)
        new_conv.append(s_c)
    mixed = jnp.concatenate([jnp.concatenate(og, 0), jnp.concatenate(cv, 0)], axis=1).astype(BF16)
    h1 = _dense(mixed, w_out, [0], tm=tm_dense, tn=_tile(d, tn, LANES), n_out=d, out_dtype=F32,
                epilogue=_ep_residual, extras=[(x, "tile")], name="out_proj")

    xn2, code, gate, counts = _router(h1, ln_moe, w_router, b_router)
    tm_e = min(256, _tile(t, 256))
    counts = counts[0, :n_experts].astype(jnp.int32)
    padded = (counts + tm_e - 1) // tm_e * tm_e
    pad_ends = jnp.cumsum(padded)
    pad_starts = (pad_ends - padded).astype(jnp.int32)
    n_blocks = -(-t * TOP_K // tm_e) + n_experts
    nu = (pad_ends[-1:] // tm_e).astype(jnp.int32)
    be = jnp.minimum(jnp.searchsorted(pad_ends, jnp.arange(n_blocks) * tm_e, side="right"),
                     n_experts - 1).astype(jnp.int32)
    code = code[:, :TOP_K].reshape(-1)
    xb = _dispatch(xn2, code, pad_starts, nu, n_blocks=n_blocks, tm=tm_e)
    act = _gmm(xb, w_gu, [0, d_ff], b_gu.reshape(n_experts, 1, d_ff2), be, nu, tm=tm_e,
               tn=_tile(d_ff, tn, LANES), n_out=d_ff, out_dtype=BF16, epilogue=_ep_swiglu, name="moe_gate_up")
    yb = _gmm(act, w_dn, [0], b_dn.reshape(n_experts, 1, d), be, nu, tm=tm_e, tn=_tile(d, tn, LANES),
              n_out=d, out_dtype=F32, epilogue=_ep_bias, name="moe_down")
    h2, xn3 = _combine(yb, h1, gate, code, pad_starts, ln_ple)

    h3 = _dense(xn3, w_ple_gate, [0], tm=_tile(t, 512), tn=_tile(d, tn, LANES), n_out=d, out_dtype=F32,
                epilogue=_ep_ple, extras=[(h2, "tile"), (p.astype(BF16), "row"), (w_ple_proj, "col")],
                name="ple")
    return h3, new_gla, new_conv


def kernel(x_prompt, x_sample, p_prompt, p_sample, state_gla, state_conv, ln_mix, w_in, w_gate_lr, b_gate, gla_norm, conv_w, conv_b, conv_ln_g, conv_ln_b, w_out, ln_moe, w_router, b_router, w_gu, b_gu, w_dn, b_dn, ln_ple, w_ple_gate, w_ple_proj, ln_final):
    bp, lp, d = x_prompt.shape
    bs, ls, _ = x_sample.shape
    depth = state_gla.shape[0]
    tp, ts = bp * lp, bs * ls
    groups = [(0, bp, lp), (tp, bs, ls)]
    h = jnp.concatenate([x_prompt.reshape(tp, d), x_sample.reshape(ts, d)], axis=0)
    zero_gla = jnp.zeros((bp,) + state_gla.shape[2:], state_gla.dtype)
    zero_conv = jnp.zeros((bp,) + state_conv.shape[2:], state_conv.dtype)
    gla_p, gla_s, conv_p, conv_s = [], [], [], []
    for i in range(depth):
        p = jnp.concatenate([p_prompt[i].reshape(tp, -1), p_sample[i].reshape(ts, -1)], axis=0)
        h, new_gla, new_conv = _layer(
            h, p, groups, [zero_gla, state_gla[i]], [zero_conv, state_conv[i]], ln_mix[i], w_in[i],
            w_gate_lr[i], b_gate[i], gla_norm[i], conv_w[i], conv_b[i], conv_ln_g[i], conv_ln_b[i],
            w_out[i], ln_moe[i], w_router[i], b_router[i], w_gu[i], b_gu[i], w_dn[i], b_dn[i],
            ln_ple[i], w_ple_gate[i], w_ple_proj[i])
        gla_p.append(new_gla[0])
        gla_s.append(new_gla[1])
        conv_p.append(new_conv[0])
        conv_s.append(new_conv[1])
    y_prompt = _rmsnorm(h, ln_final, F32, row0=0, rows=tp).reshape(bp, lp, d)
    y_sample = _rmsnorm(h, ln_final, F32, row0=tp, rows=ts).reshape(bs, ls, d)
    return (y_prompt, y_sample, jnp.stack(gla_p, 0), jnp.stack(conv_p, 0),
            jnp.stack(gla_s, 0), jnp.stack(conv_s, 0))
```

```python
import functools

import jax
import jax.numpy as jnp
from jax import lax
from jax.experimental import pallas as pl
from jax.experimental.pallas import tpu as pltpu

EPS = 1e-6
TOP_K = 4
GATE_NORMALIZER = 16.0
GLA_CHUNK = 16
SWIGLU_LIMIT = 7.0
SWIGLU_ALPHA = 1.702

LANES = 128
VMEM_LIMIT_BYTES = 60 * 2**20
VMEM_BUDGET_BYTES = 52 * 2**20
RANK_BITS = 16

F32 = jnp.float32
BF16 = jnp.bfloat16


def _tile(n, pref, align=8):
    if n <= pref:
        return n
    for t in range(pref, 0, -1):
        if n % t == 0 and t % align == 0:
            return t
    return n


def _params(sem):
    return pltpu.CompilerParams(dimension_semantics=sem, vmem_limit_bytes=VMEM_LIMIT_BYTES)


def _rms_body(x_ref, g_ref, o_ref):
    x = x_ref[...]
    y = x * lax.rsqrt(jnp.mean(x * x, axis=-1, keepdims=True) + EPS)
    o_ref[...] = (y * g_ref[...]).astype(o_ref.dtype)


def _rmsnorm(x, g, out_dtype, row0=0, rows=None):
    rows = x.shape[0] if rows is None else rows
    d = x.shape[1]
    tm = _tile(rows, 512)
    assert row0 % tm == 0
    off = row0 // tm
    return pl.pallas_call(
        _rms_body,
        grid=(rows // tm,),
        in_specs=[pl.BlockSpec((tm, d), lambda i: (i + off, 0)),
                  pl.BlockSpec((1, d), lambda i: (0, 0))],
        out_specs=pl.BlockSpec((tm, d), lambda i: (i, 0)),
        out_shape=jax.ShapeDtypeStruct((rows, d), out_dtype),
        compiler_params=_params(("arbitrary",)),
        name="rmsnorm",
    )(x, g.reshape(1, d))


def _gmm_body(be_ref, nu_ref, *refs, n_w, n_b, n_extra, epilogue, cast_rows):
    x_ref = refs[0]
    w_refs = refs[1:1 + n_w]
    b_refs = refs[1 + n_w:1 + n_w + n_b]
    e_refs = refs[1 + n_w + n_b:1 + n_w + n_b + n_extra]
    o_ref = refs[1 + n_w + n_b + n_extra]
    wb_refs = refs[2 + n_w + n_b + n_extra:]
    i = pl.program_id(1)
    valid = i < nu_ref[0]
    prev = jnp.maximum(i - 1, 0)
    changed = jnp.logical_or(i == 0, be_ref[i] != be_ref[prev])
    k = w_refs[0].shape[0]

    @pl.when(jnp.logical_and(valid, changed))
    def _():
        def cast(r, carry):
            sl = pl.ds(pl.multiple_of(r * cast_rows, cast_rows), cast_rows)
            for w_ref, wb_ref in zip(w_refs, wb_refs):
                wb_ref[sl, :] = w_ref[sl, :].astype(BF16)
            return carry
        lax.fori_loop(0, k // cast_rows, cast, 0)

    @pl.when(valid)
    def _():
        x = x_ref[...]
        accs = [jnp.dot(x, wb_ref[...], preferred_element_type=F32) for wb_ref in wb_refs]
        o_ref[...] = epilogue(accs, [b[...] for b in b_refs], [e[...] for e in e_refs]).astype(o_ref.dtype)

    @pl.when(jnp.logical_not(valid))
    def _():
        o_ref[...] = jnp.zeros(o_ref.shape, o_ref.dtype)


def _gmm(x, w, col_offsets, bias, be, nu, *, tm, tn, n_out, out_dtype, epilogue, extras=(), name):
    r, k = x.shape
    nb = r // tm
    assert r % tm == 0 and n_out % tn == 0 and all(o % tn == 0 for o in col_offsets)
    n_w = len(col_offsets)

    def row(i, nu_ref):
        return jnp.minimum(i, nu_ref[0] - 1)

    in_specs = [pl.BlockSpec((tm, k), lambda j, i, be_r, nu_r: (row(i, nu_r), 0))]
    args = [x]
    for off in col_offsets:
        ob = off // tn
        in_specs.append(pl.BlockSpec((None, k, tn),
                                     lambda j, i, be_r, nu_r, ob=ob: (be_r[row(i, nu_r)], 0, j + ob)))
        args.append(w)
    n_b = 0
    if bias is not None:
        for off in col_offsets:
            ob = off // tn
            in_specs.append(pl.BlockSpec((None, 1, tn),
                                         lambda j, i, be_r, nu_r, ob=ob: (be_r[row(i, nu_r)], 0, j + ob)))
            args.append(bias)
            n_b += 1
    for arr, kind in extras:
        if kind == "tile":
            in_specs.append(pl.BlockSpec((tm, tn), lambda j, i, be_r, nu_r: (row(i, nu_r), j)))
        elif kind == "row":
            in_specs.append(pl.BlockSpec((tm, arr.shape[1]), lambda j, i, be_r, nu_r: (row(i, nu_r), 0)))
        else:
            in_specs.append(pl.BlockSpec((arr.shape[0], tn), lambda j, i, be_r, nu_r: (0, j)))
        args.append(arr)
    vmem = (2 * tm * k * 2 + n_w * (2 * k * tn * 4 + k * tn * 2) + n_w * tm * tn * 4
            + 2 * tm * tn * jnp.dtype(out_dtype).itemsize)
    for arr, kind in extras:
        blk = {"tile": tm * tn, "row": tm * arr.shape[1], "col": arr.shape[0] * tn}[kind]
        vmem += 2 * blk * arr.dtype.itemsize
    assert vmem <= VMEM_BUDGET_BYTES, (name, vmem)
    cast_rows = _tile(k, 256)
    body = functools.partial(_gmm_body, n_w=n_w, n_b=n_b, n_extra=len(extras), epilogue=epilogue,
                             cast_rows=cast_rows)
    return pl.pallas_call(
        body,
        grid_spec=pltpu.PrefetchScalarGridSpec(
            num_scalar_prefetch=2,
            grid=(n_out // tn, nb),
            in_specs=in_specs,
            out_specs=pl.BlockSpec((tm, tn), lambda j, i, be_r, nu_r: (i, j)),
            scratch_shapes=[pltpu.VMEM((k, tn), BF16) for _ in range(n_w)]),
        out_shape=jax.ShapeDtypeStruct((r, n_out), out_dtype),
        compiler_params=_params(("arbitrary", "arbitrary")),
        name=name,
    )(be, nu, *args)


def _dense(x, w2d, col_offsets, *, tm, tn, n_out, out_dtype, epilogue, extras=(), name):
    nb = x.shape[0] // tm
    return _gmm(x, w2d[None], col_offsets, None, jnp.zeros((nb,), jnp.int32),
                jnp.full((1,), nb, jnp.int32), tm=tm, tn=tn, n_out=n_out, out_dtype=out_dtype,
                epilogue=epilogue, extras=extras, name=name)


def _sigmoid(x):
    return 1.0 / (1.0 + jnp.exp(-x))


def _ep_plain(accs, biases, extras):
    return accs[0]


def _ep_glu(accs, biases, extras):
    return accs[0] * _sigmoid(accs[1])


def _ep_residual(accs, biases, extras):
    return accs[0] + extras[0]


def _ep_swiglu(accs, biases, extras):
    g = jnp.minimum(accs[0] + biases[0], SWIGLU_LIMIT)
    up = jnp.clip(accs[1] + biases[1], -SWIGLU_LIMIT, SWIGLU_LIMIT)
    return g * _sigmoid(SWIGLU_ALPHA * g) * (up + 1.0)


def _ep_bias(accs, biases, extras):
    return accs[0] + biases[0]


def _ep_ple(accs, biases, extras):
    h, p, wp = extras
    pp = jnp.dot(p, wp.astype(BF16), preferred_element_type=F32)
    return h + _sigmoid(accs[0]) * pp


def _loga_body(x_ref, wa_ref, wg_ref, bg_ref, o_ref):
    a = jnp.dot(x_ref[...], wa_ref[...], preferred_element_type=F32)
    z = jnp.dot(a.astype(BF16), wg_ref[...], preferred_element_type=F32) + bg_ref[...]
    log_sig = jnp.minimum(z, 0.0) - jnp.log1p(jnp.exp(-jnp.abs(z)))
    o_ref[...] = log_sig / GATE_NORMALIZER


def _log_decay(xn, w_alr, w_gate_lr, b_gate):
    t, d = xn.shape
    rank, qk = w_gate_lr.shape
    assert rank <= LANES
    wa = jnp.zeros((d, LANES), BF16).at[:, :rank].set(w_alr.astype(BF16))
    wg = jnp.zeros((LANES, qk), BF16).at[:rank, :].set(w_gate_lr.astype(BF16))
    tm = _tile(t, 512)
    return pl.pallas_call(
        _loga_body,
        grid=(t // tm,),
        in_specs=[pl.BlockSpec((tm, d), lambda i: (i, 0)),
                  pl.BlockSpec((d, LANES), lambda i: (0, 0)),
                  pl.BlockSpec((LANES, qk), lambda i: (0, 0)),
                  pl.BlockSpec((1, qk), lambda i: (0, 0))],
        out_specs=pl.BlockSpec((tm, qk), lambda i: (i, 0)),
        out_shape=jax.ShapeDtypeStruct((t, qk), F32),
        compiler_params=_params(("arbitrary",)),
        name="gla_log_decay",
    )(xn, wa, wg, b_gate.reshape(1, qk))


def _split3(x):
    hi = x.astype(BF16)
    r1 = x - hi.astype(F32)
    mid = r1.astype(BF16)
    lo = (r1 - mid.astype(F32)).astype(BF16)
    return hi, mid, lo


def _gla_body(q_ref, k_ref, v_ref, r_ref, la_ref, s0_ref, gn_ref, og_ref, sn_ref, s_ref,
              *, chunk, n_sub, lc, scale):
    c = pl.program_id(1)

    @pl.when(c == 0)
    def _():
        s_ref[...] = s0_ref[...]

    nb, heads, dk, dv = s_ref.shape
    assert n_sub <= LANES
    rows = lax.broadcasted_iota(jnp.int32, (lc, lc), 0)
    cols = lax.broadcasted_iota(jnp.int32, (lc, lc), 1)
    same_chunk = (rows // chunk) == (cols // chunk)
    causal = jnp.logical_and(same_chunk, rows >= cols)
    tril = causal.astype(BF16)
    chunk_sum = same_chunk.astype(BF16)
    chunk_col = (lax.broadcasted_iota(jnp.int32, (lc, LANES), 0) // chunk
                 == lax.broadcasted_iota(jnp.int32, (lc, LANES), 1)).astype(BF16)
    tdims = (((0,), (0,)), ((), ()))

    chains = []
    for bb in range(nb):
        sl = slice(bb * lc, (bb + 1) * lc)
        for h in range(heads):
            kc = slice(h * dk, (h + 1) * dk)
            vc = slice(h * dv, (h + 1) * dv)
            q = q_ref[sl, kc] * scale
            k = k_ref[sl, kc]
            v = v_ref[sl, vc].astype(BF16)
            pieces = _split3(la_ref[sl, kc])
            b = sum(jnp.dot(tril, p, preferred_element_type=F32) for p in pieces)
            b_end = sum(jnp.dot(chunk_sum, p, preferred_element_type=F32) for p in pieces)
            b_end_col = sum(lax.dot_general(p, chunk_col, tdims, preferred_element_type=F32)
                            for p in pieces)
            decay = jnp.exp(b_end_col)
            qe = (q * jnp.exp(b)).astype(BF16)
            ke = (k * jnp.exp(-b)).astype(BF16)
            kd = (k * jnp.exp(b_end - b)).astype(BF16)
            att = lax.dot_general(qe, ke, (((1,), (1,)), ((), ())), preferred_element_type=F32)
            att = jnp.where(causal, att, 0.0).astype(BF16)
            o_intra = jnp.dot(att, v, preferred_element_type=F32)
            chains.append((bb, h, vc, qe, kd, v, decay, o_intra))

    for i in range(n_sub):
        cs = slice(i * chunk, (i + 1) * chunk)
        for bb, h, vc, qe, kd, v, decay, o_intra in chains:
            s = s_ref[bb, h]
            o = jnp.dot(qe[cs], s.astype(BF16), preferred_element_type=F32) + o_intra[cs]
            s_ref[bb, h] = (decay[:, i:i + 1] * s
                            + lax.dot_general(kd[cs], v[cs], tdims, preferred_element_type=F32))
            og_ref[bb * lc + i * chunk:bb * lc + (i + 1) * chunk, vc] = o

    o = og_ref[...]
    r = r_ref[...]
    for h in range(heads):
        vc = slice(h * dv, (h + 1) * dv)
        oh = o[:, vc]
        oh = oh * lax.rsqrt(jnp.mean(oh * oh, axis=-1, keepdims=True) + EPS) * gn_ref[:, vc]
        og_ref[:, vc] = (oh * (r[:, vc] * _sigmoid(r[:, vc]))).astype(og_ref.dtype)

    @pl.when(c == pl.num_programs(1) - 1)
    def _():
        sn_ref[...] = s_ref[...]


def _gla(qkvr, log_a, s0, gla_norm, *, row0, batch, length):
    _, heads, dk, dv = s0.shape
    qk, d_gla = heads * dk, heads * dv
    chunk = min(GLA_CHUNK, length)
    lc = _tile(length, 256, chunk)
    nc = length // lc
    nb = 2 if (nc == 1 and batch % 2 == 0) else 1
    rows = nb * lc
    assert length % chunk == 0 and row0 % rows == 0 and (2 * qk) % d_gla == 0
    rb0 = row0 // rows
    v_blk = 2 * qk // d_gla
    body = functools.partial(_gla_body, chunk=chunk, n_sub=lc // chunk, lc=lc, scale=dk ** -0.5)
    rowblk = lambda b, c: rb0 + b * nc + c
    og, s_new = pl.pallas_call(
        body,
        grid=(batch // nb, nc),
        in_specs=[pl.BlockSpec((rows, qk), lambda b, c: (rowblk(b, c), 0)),
                  pl.BlockSpec((rows, qk), lambda b, c: (rowblk(b, c), 1)),
                  pl.BlockSpec((rows, d_gla), lambda b, c: (rowblk(b, c), v_blk)),
                  pl.BlockSpec((rows, d_gla), lambda b, c: (rowblk(b, c), v_blk + 1)),
                  pl.BlockSpec((rows, qk), lambda b, c: (rowblk(b, c), 0)),
                  pl.BlockSpec((nb, heads, dk, dv), lambda b, c: (b, 0, 0, 0)),
                  pl.BlockSpec((1, d_gla), lambda b, c: (0, 0))],
        out_specs=[pl.BlockSpec((rows, d_gla), lambda b, c: (b * nc + c, 0)),
                   pl.BlockSpec((nb, heads, dk, dv), lambda b, c: (b, 0, 0, 0))],
        out_shape=[jax.ShapeDtypeStruct((batch * length, d_gla), F32),
                   jax.ShapeDtypeStruct((batch, heads, dk, dv), F32)],
        scratch_shapes=[pltpu.VMEM((nb, heads, dk, dv), F32)],
        compiler_params=_params(("arbitrary", "arbitrary")),
        name="gla",
    )(qkvr, qkvr, qkvr, qkvr, log_a, s0, gla_norm.reshape(1, d_gla))
    return og, s_new


def _conv_body(u_ref, st_ref, w_ref, b_ref, g_ref, beta_ref, c_ref, sn_ref, ext_ref, y_ref,
               *, width, lc, pad, row_tile, ch_tile):
    c = pl.program_id(1)
    hist = width - 1
    ch = u_ref.shape[1]

    @pl.when(c == 0)
    def _():
        if pad:
            ext_ref[0:pad, :] = jnp.zeros((pad, ch), F32)
        ext_ref[pad:pad + hist, :] = st_ref[...]

    base = pad + hist
    ext_ref[base:base + lc, :] = u_ref[...]

    def taps(ci, carry):
        cs = pl.ds(pl.multiple_of(ci * ch_tile, ch_tile), ch_tile)
        for r0 in range(0, lc, row_tile):
            acc = jnp.zeros((row_tile, ch_tile), F32) + b_ref[:, cs]
            for j in range(width):
                acc = acc + w_ref[j:j + 1, cs] * ext_ref[pad + j + r0:pad + j + r0 + row_tile, cs]
            y_ref[r0:r0 + row_tile, cs] = acc
        return carry
    lax.fori_loop(0, ch // ch_tile, taps, 0)

    acc = y_ref[...]
    mu = jnp.mean(acc, axis=-1, keepdims=True)
    cen = acc - mu
    var = jnp.mean(cen * cen, axis=-1, keepdims=True)
    y = cen * lax.rsqrt(var + EPS) * g_ref[...] + beta_ref[...]
    c_ref[...] = (y * _sigmoid(y)).astype(c_ref.dtype)
    tail = ext_ref[pad + lc:pad + lc + hist, :]
    ext_ref[pad:pad + hist, :] = tail

    @pl.when(c == pl.num_programs(1) - 1)
    def _():
        sn_ref[...] = tail


def _conv(u, state, conv_w, conv_b, ln_g, ln_b, *, row0, batch, length):
    ch = u.shape[1]
    width = conv_w.shape[0]
    hist = width - 1
    lc = _tile(length, 128)
    nc = length // lc
    assert row0 % lc == 0 and lc % 8 == 0
    rb0 = row0 // lc
    pad = (-hist) % 8
    row_tile = _tile(lc, 32)
    body = functools.partial(_conv_body, width=width, lc=lc, pad=pad, row_tile=row_tile,
                             ch_tile=_tile(ch, 2 * LANES, LANES))
    vec = lambda a: a.reshape(1, ch)
    return pl.pallas_call(
        body,
        grid=(batch, nc),
        in_specs=[pl.BlockSpec((lc, ch), lambda b, c: (rb0 + b * nc + c, 0)),
                  pl.BlockSpec((None, hist, ch), lambda b, c: (b, 0, 0)),
                  pl.BlockSpec((width, ch), lambda b, c: (0, 0)),
                  pl.BlockSpec((1, ch), lambda b, c: (0, 0)),
                  pl.BlockSpec((1, ch), lambda b, c: (0, 0)),
                  pl.BlockSpec((1, ch), lambda b, c: (0, 0))],
        out_specs=[pl.BlockSpec((lc, ch), lambda b, c: (b * nc + c, 0)),
                   pl.BlockSpec((None, hist, ch), lambda b, c: (b, 0, 0))],
        out_shape=[jax.ShapeDtypeStruct((batch * length, ch), F32),
                   jax.ShapeDtypeStruct((batch, hist, ch), F32)],
        scratch_shapes=[pltpu.VMEM((pad + hist + lc, ch), F32), pltpu.VMEM((lc, ch), F32)],
        compiler_params=_params(("arbitrary", "arbitrary")),
        name="conformer_conv",
    )(u, state, conv_w, vec(conv_b), vec(ln_g), vec(ln_b))


def _router_body(h_ref, g_ref, w_ref, b_ref, xn_ref, code_ref, gate_ref, cnt_ref, carry_ref,
                 *, n_experts):
    i = pl.program_id(0)

    @pl.when(i == 0)
    def _():
        carry_ref[...] = jnp.zeros(carry_ref.shape, F32)

    h = h_ref[...]
    xn = h * lax.rsqrt(jnp.mean(h * h, axis=-1, keepdims=True) + EPS) * g_ref[...]
    xn_ref[...] = xn
    tm = h.shape[0]
    x_hi = xn.astype(BF16)
    x_lo = (xn - x_hi.astype(F32)).astype(BF16)
    w = w_ref[...]
    w_hi = w.astype(BF16)
    w_lo = (w - w_hi.astype(F32)).astype(BF16)
    logits = (jnp.dot(x_hi, w_hi, preferred_element_type=F32)
              + jnp.dot(x_lo, w_hi, preferred_element_type=F32)
              + jnp.dot(x_hi, w_lo, preferred_element_type=F32)) + b_ref[...]
    lane = lax.broadcasted_iota(jnp.int32, (tm, LANES), 1)
    lane_f = lane.astype(F32)
    work = jnp.where(lane < n_experts, logits, -jnp.inf)
    tops, hots = [], []
    for _ in range(TOP_K):
        m = jnp.max(work, axis=-1, keepdims=True)
        idx = jnp.min(jnp.where(work == m, lane_f, float(LANES)), axis=-1, keepdims=True)
        hot = lane_f == idx
        tops.append(m)
        hots.append(hot)
        work = jnp.where(hot, -jnp.inf, work)
    exps = [jnp.exp(t - tops[0]) for t in tops]
    denom = sum(exps)
    chosen = sum(hot.astype(F32) for hot in hots)
    rows = lax.broadcasted_iota(jnp.int32, (tm, tm), 0)
    cols = lax.broadcasted_iota(jnp.int32, (tm, tm), 1)
    before = (rows > cols).astype(BF16)
    seen = jnp.dot(before, chosen.astype(BF16), preferred_element_type=F32) + carry_ref[0:1, :]
    code = jnp.zeros((tm, LANES), jnp.int32)
    gate = jnp.zeros((tm, LANES), F32)
    for kk in range(TOP_K):
        hot_f = hots[kk].astype(F32)
        rank = jnp.sum(seen * hot_f, axis=-1, keepdims=True)
        expert = jnp.sum(lane_f * hot_f, axis=-1, keepdims=True)
        packed = expert.astype(jnp.int32) * (1 << RANK_BITS) + rank.astype(jnp.int32)
        code = jnp.where(lane == kk, packed, code)
        gate = jnp.where(lane == kk, exps[kk] / denom, gate)
    code_ref[...] = code
    gate_ref[...] = gate
    total = carry_ref[...] + jnp.sum(chosen, axis=0, keepdims=True)
    carry_ref[...] = total
    cnt_ref[...] = total


def _router(h, ln_g, w_router, b_router):
    t, d = h.shape
    n_experts = w_router.shape[1]
    assert n_experts <= LANES and t < (1 << RANK_BITS)
    wr = jnp.zeros((d, LANES), F32).at[:, :n_experts].set(w_router)
    br = jnp.zeros((1, LANES), F32).at[0, :n_experts].set(b_router)
    tm = _tile(t, 256)
    body = functools.partial(_router_body, n_experts=n_experts)
    return pl.pallas_call(
        body,
        grid=(t // tm,),
        in_specs=[pl.BlockSpec((tm, d), lambda i: (i, 0)),
                  pl.BlockSpec((1, d), lambda i: (0, 0)),
                  pl.BlockSpec((d, LANES), lambda i: (0, 0)),
                  pl.BlockSpec((1, LANES), lambda i: (0, 0))],
        out_specs=[pl.BlockSpec((tm, d), lambda i: (i, 0)),
                   pl.BlockSpec((tm, LANES), lambda i: (i, 0)),
                   pl.BlockSpec((tm, LANES), lambda i: (i, 0)),
                   pl.BlockSpec((8, LANES), lambda i: (0, 0))],
        out_shape=[jax.ShapeDtypeStruct((t, d), F32),
                   jax.ShapeDtypeStruct((t, LANES), jnp.int32),
                   jax.ShapeDtypeStruct((t, LANES), F32),
                   jax.ShapeDtypeStruct((8, LANES), F32)],
        scratch_shapes=[pltpu.VMEM((8, LANES), F32)],
        compiler_params=_params(("arbitrary",)),
        name="moe_router",
    )(h, ln_g.reshape(1, d), wr, br)


def _slot(code_ref, ps_ref, a):
    p = code_ref[a]
    return ps_ref[p >> RANK_BITS] + (p & ((1 << RANK_BITS) - 1))


def _dispatch_body(code_ref, ps_ref, cnt_ref, nu_ref, x_hbm, o_ref, tok_ref, buf_ref, sem,
                   *, n_tokens, n_experts, tm):
    i = pl.program_id(0)
    nu = nu_ref[0]

    def row_copy(blk, r):
        half = blk % 2
        return pltpu.make_async_copy(x_hbm.at[pl.ds(tok_ref[blk * tm + r], 1), :],
                                     buf_ref.at[half, pl.ds(r, 1), :], sem.at[half])

    def start_block(blk):
        def start(r, carry):
            row_copy(blk, r).start()
            return carry
        lax.fori_loop(0, tm, start, 0, unroll=8)

    @pl.when(i == 0)
    def _():
        def pad_expert(e, carry):
            first = ps_ref[e] + cnt_ref[e]
            last = ps_ref[e] + (cnt_ref[e] + tm - 1) // tm * tm

            def clear(s):
                tok_ref[s] = 0
                return s + 1
            lax.while_loop(lambda s: s < last, clear, first)
            return carry
        lax.fori_loop(0, n_experts, pad_expert, 0)

        def place(t):
            for kk in range(TOP_K):
                tok_ref[_slot(code_ref, ps_ref, t * TOP_K + kk)] = t
            return t + 1
        lax.while_loop(lambda t: t < n_tokens, place, 0)
        start_block(0)

    @pl.when(i + 1 < nu)
    def _():
        start_block(i + 1)

    @pl.when(i < nu)
    def _():
        def wait(r, carry):
            row_copy(i, r).wait()
            return carry
        lax.fori_loop(0, tm, wait, 0, unroll=8)
        o_ref[...] = buf_ref[i % 2].astype(o_ref.dtype)

    @pl.when(i >= nu)
    def _():
        o_ref[...] = jnp.zeros(o_ref.shape, o_ref.dtype)


def _dispatch(xn, code, pad_starts, counts, nu, *, n_blocks, tm):
    t, d = xn.shape
    body = functools.partial(_dispatch_body, n_tokens=t, n_experts=counts.shape[0], tm=tm)
    return pl.pallas_call(
        body,
        grid_spec=pltpu.PrefetchScalarGridSpec(
            num_scalar_prefetch=4,
            grid=(n_blocks,),
            in_specs=[pl.BlockSpec(memory_space=pl.ANY)],
            out_specs=pl.BlockSpec((tm, d), lambda i, c_r, p_r, n_r, nu_r: (i, 0)),
            scratch_shapes=[pltpu.SMEM((n_blocks * tm,), jnp.int32),
                            pltpu.VMEM((2, tm, d), F32),
                            pltpu.SemaphoreType.DMA((2,))]),
        out_shape=jax.ShapeDtypeStruct((n_blocks * tm, d), BF16),
        compiler_params=_params(("arbitrary",)),
        name="moe_dispatch",
    )(code, pad_starts, counts, nu, xn)


def _combine_body(code_ref, ps_ref, y_hbm, h_ref, gate_ref, g_ref, h_out_ref, xn_ref, buf_ref, sem, *, tm):
    i = pl.program_id(0)

    def row_copy(r, kk):
        slot = _slot(code_ref, ps_ref, (i * tm + r) * TOP_K + kk)
        return pltpu.make_async_copy(y_hbm.at[pl.ds(slot, 1), :], buf_ref.at[kk, pl.ds(r, 1), :], sem)

    def start(r, carry):
        for kk in range(TOP_K):
            row_copy(r, kk).start()
        return carry
    lax.fori_loop(0, tm, start, 0)

    def wait(r, carry):
        for kk in range(TOP_K):
            row_copy(r, kk).wait()
        return carry
    lax.fori_loop(0, tm, wait, 0)

    gate = gate_ref[...]
    h = h_ref[...]
    for kk in range(TOP_K):
        h = h + gate[:, kk:kk + 1] * buf_ref[kk]
    h_out_ref[...] = h
    xn = h * lax.rsqrt(jnp.mean(h * h, axis=-1, keepdims=True) + EPS) * g_ref[...]
    xn_ref[...] = xn.astype(xn_ref.dtype)


def _combine(yb, h, gate, code, pad_starts, ln_g):
    t, d = h.shape
    tm = _tile(t, 128)
    body = functools.partial(_combine_body, tm=tm)
    return pl.pallas_call(
        body,
        grid_spec=pltpu.PrefetchScalarGridSpec(
            num_scalar_prefetch=2,
            grid=(t // tm,),
            in_specs=[pl.BlockSpec(memory_space=pl.ANY),
                      pl.BlockSpec((tm, d), lambda i, c_r, p_r: (i, 0)),
                      pl.BlockSpec((tm, LANES), lambda i, c_r, p_r: (i, 0)),
                      pl.BlockSpec((1, d), lambda i, c_r, p_r: (0, 0))],
            out_specs=[pl.BlockSpec((tm, d), lambda i, c_r, p_r: (i, 0)),
                       pl.BlockSpec((tm, d), lambda i, c_r, p_r: (i, 0))],
            scratch_shapes=[pltpu.VMEM((TOP_K, tm, d), F32),
                            pltpu.SemaphoreType.DMA(())]),
        out_shape=[jax.ShapeDtypeStruct((t, d), F32),
                   jax.ShapeDtypeStruct((t, d), BF16)],
        compiler_params=_params(("arbitrary",)),
        name="moe_combine",
    )(code, pad_starts, yb, h, gate, ln_g.reshape(1, d))


def _layer(x, p, groups, s_gla, s_conv, ln_mix, w_in, w_gate_lr, b_gate, gla_norm, conv_w, conv_b,
           conv_ln_g, conv_ln_b, w_out, ln_moe, w_router, b_router, w_gu, b_gu, w_dn, b_dn,
           ln_ple, w_ple_gate, w_ple_proj):
    t, d = x.shape
    heads, dk, dv = s_gla[0].shape[1:]
    qk, d_gla = heads * dk, heads * dv
    rank = w_gate_lr.shape[0]
    d_conv = conv_w.shape[1]
    n_experts, _, d_ff2 = w_gu.shape
    d_ff = d_ff2 // 2
    n_qkvr = 2 * qk + 2 * d_gla
    glu0 = n_qkvr + rank
    tm_dense = _tile(t, 1024)
    tn = 512

    xn1 = _rmsnorm(x, ln_mix, BF16)
    qkvr = _dense(xn1, w_in[:, :n_qkvr], [0], tm=tm_dense, tn=_tile(n_qkvr, tn, LANES), n_out=n_qkvr, out_dtype=F32,
                  epilogue=_ep_plain, name="in_proj_qkvr")
    u = _dense(xn1, w_in[:, glu0:], [0, d_conv], tm=tm_dense, tn=_tile(d_conv, tn // 2, LANES), n_out=d_conv,
               out_dtype=F32, epilogue=_ep_glu, name="in_proj_glu")
    log_a = _log_decay(xn1, w_in[:, n_qkvr:glu0], w_gate_lr, b_gate)
    og, cv, new_gla, new_conv = [], [], [], []
    for (row0, batch, length), sg, sc in zip(groups, s_gla, s_conv):
        o_g, s_g = _gla(qkvr, log_a, sg, gla_norm, row0=row0, batch=batch, length=length)
        o_c, s_c = _conv(u, sc, conv_w, conv_b, conv_ln_g, conv_ln_b, row0=row0, batch=batch, length=length)
        og.append(o_g)
        cv.append(o_c)
        new_gla.append(s_g)
        new_conv.append(s_c)
    mixed = jnp.concatenate([jnp.concatenate(og, 0), jnp.concatenate(cv, 0)], axis=1).astype(BF16)
    h1 = _dense(mixed, w_out, [0], tm=tm_dense, tn=_tile(d, tn, LANES), n_out=d, out_dtype=F32,
                epilogue=_ep_residual, extras=[(x, "tile")], name="out_proj")

    xn2, code, gate, counts = _router(h1, ln_moe, w_router, b_router)
    tm_e = _tile(t, 512)
    counts = counts[0, :n_experts].astype(jnp.int32)
    padded = (counts + tm_e - 1) // tm_e * tm_e
    pad_ends = jnp.cumsum(padded)
    pad_starts = (pad_ends - padded).astype(jnp.int32)
    n_blocks = -(-t * TOP_K // tm_e) + n_experts
    nu = (pad_ends[-1:] // tm_e).astype(jnp.int32)
    block_rows = jnp.arange(n_blocks, dtype=jnp.int32) * tm_e
    be = jnp.minimum(jnp.sum(pad_ends[None, :] <= block_rows[:, None], axis=1),
                     n_experts - 1).astype(jnp.int32)
    code = code[:, :TOP_K].reshape(-1)
    xb = _dispatch(xn2, code, pad_starts, counts, nu, n_blocks=n_blocks, tm=tm_e)
    act = _gmm(xb, w_gu, [0, d_ff], b_gu.reshape(n_experts, 1, d_ff2), be, nu, tm=tm_e,
               tn=_tile(d_ff, tn, LANES), n_out=d_ff, out_dtype=BF16, epilogue=_ep_swiglu, name="moe_gate_up")
    yb = _gmm(act, w_dn, [0], b_dn.reshape(n_experts, 1, d), be, nu, tm=tm_e, tn=_tile(d, tn, LANES),
              n_out=d, out_dtype=F32, epilogue=_ep_bias, name="moe_down")
    h2, xn3 = _combine(yb, h1, gate, code, pad_starts, ln_ple)

    h3 = _dense(xn3, w_ple_gate, [0], tm=_tile(t, 512), tn=_tile(d, tn, LANES), n_out=d, out_dtype=F32,
                epilogue=_ep_ple, extras=[(h2, "tile"), (p.astype(BF16), "row"), (w_ple_proj, "col")],
                name="ple")
    return h3, new_gla, new_conv


def kernel(x_prompt, x_sample, p_prompt, p_sample, state_gla, state_conv, ln_mix, w_in, w_gate_lr, b_gate, gla_norm, conv_w, conv_b, conv_ln_g, conv_ln_b, w_out, ln_moe, w_router, b_router, w_gu, b_gu, w_dn, b_dn, ln_ple, w_ple_gate, w_ple_proj, ln_final):
    bp, lp, d = x_prompt.shape
    bs, ls, _ = x_sample.shape
    depth = state_gla.shape[0]
    tp, ts = bp * lp, bs * ls
    groups = [(0, bp, lp), (tp, bs, ls)]
    h = jnp.concatenate([x_prompt.reshape(tp, d), x_sample.reshape(ts, d)], axis=0)
    zero_gla = jnp.zeros((bp,) + state_gla.shape[2:], state_gla.dtype)
    zero_conv = jnp.zeros((bp,) + state_conv.shape[2:], state_conv.dtype)
    gla_p, gla_s, conv_p, conv_s = [], [], [], []
    for i in range(depth):
        p = jnp.concatenate([p_prompt[i].reshape(tp, -1), p_sample[i].reshape(ts, -1)], axis=0)
        h, new_gla, new_conv = _layer(
            h, p, groups, [zero_gla, state_gla[i]], [zero_conv, state_conv[i]], ln_mix[i], w_in[i],
            w_gate_lr[i], b_gate[i], gla_norm[i], conv_w[i], conv_b[i], conv_ln_g[i], conv_ln_b[i],
            w_out[i], ln_moe[i], w_router[i], b_router[i], w_gu[i], b_gu[i], w_dn[i], b_dn[i],
            ln_ple[i], w_ple_gate[i], w_ple_proj[i])
        gla_p.append(new_gla[0])
        gla_s.append(new_gla[1])
        conv_p.append(new_conv[0])
        conv_s.append(new_conv[1])
    y_prompt = _rmsnorm(h, ln_final, F32, row0=0, rows=tp).reshape(bp, lp, d)
    y_sample = _rmsnorm(h, ln_final, F32, row0=tp, rows=ts).reshape(bs, ls, d)
    return (y_prompt, y_sample, jnp.stack(gla_p, 0), jnp.stack(conv_p, 0),
            jnp.stack(gla_s, 0), jnp.stack(conv_s, 0))
```

```python
import functools

import jax
import jax.numpy as jnp
from jax import lax
from jax.experimental import pallas as pl
from jax.experimental.pallas import tpu as pltpu

EPS = 1e-6
TOP_K = 4
GATE_NORMALIZER = 16.0
GLA_CHUNK = 16
SWIGLU_LIMIT = 7.0
SWIGLU_ALPHA = 1.702

LANES = 128
VMEM_LIMIT_BYTES = 60 * 2**20
VMEM_BUDGET_BYTES = 54 * 2**20
RANK_BITS = 16
EXPERT_ROWS = 256
GROUP_BLOCKS = 5

F32 = jnp.float32
BF16 = jnp.bfloat16


def _tile(n, pref, align=8):
    if n <= pref:
        return n
    for t in range(pref, 0, -1):
        if n % t == 0 and t % align == 0:
            return t
    return n


def _params(sem):
    return pltpu.CompilerParams(dimension_semantics=sem, vmem_limit_bytes=VMEM_LIMIT_BYTES)


def _rms_body(x_ref, g_ref, o_ref):
    x = x_ref[...]
    y = x * lax.rsqrt(jnp.mean(x * x, axis=-1, keepdims=True) + EPS)
    o_ref[...] = (y * g_ref[...]).astype(o_ref.dtype)


def _rmsnorm(x, g, out_dtype, row0=0, rows=None):
    rows = x.shape[0] if rows is None else rows
    d = x.shape[1]
    tm = _tile(rows, 512)
    assert row0 % tm == 0
    off = row0 // tm
    return pl.pallas_call(
        _rms_body,
        grid=(rows // tm,),
        in_specs=[pl.BlockSpec((tm, d), lambda i: (i + off, 0)),
                  pl.BlockSpec((1, d), lambda i: (0, 0))],
        out_specs=pl.BlockSpec((tm, d), lambda i: (i, 0)),
        out_shape=jax.ShapeDtypeStruct((rows, d), out_dtype),
        compiler_params=_params(("arbitrary",)),
        name="rmsnorm",
    )(x, g.reshape(1, d))


def _gmm_body(be_ref, nu_ref, *refs, n_w, n_b, n_extra, epilogue, cast_rows):
    x_ref = refs[0]
    w_refs = refs[1:1 + n_w]
    b_refs = refs[1 + n_w:1 + n_w + n_b]
    e_refs = refs[1 + n_w + n_b:1 + n_w + n_b + n_extra]
    o_ref = refs[1 + n_w + n_b + n_extra]
    wb_refs = refs[2 + n_w + n_b + n_extra:]
    i = pl.program_id(1)
    valid = i < nu_ref[0]
    prev = jnp.maximum(i - 1, 0)
    changed = jnp.logical_or(i == 0, be_ref[i] != be_ref[prev])
    k = w_refs[0].shape[0]

    @pl.when(jnp.logical_and(valid, changed))
    def _():
        def cast(r, carry):
            sl = pl.ds(pl.multiple_of(r * cast_rows, cast_rows), cast_rows)
            for w_ref, wb_ref in zip(w_refs, wb_refs):
                wb_ref[sl, :] = w_ref[sl, :].astype(BF16)
            return carry
        lax.fori_loop(0, k // cast_rows, cast, 0)

    @pl.when(valid)
    def _():
        x = x_ref[...]
        accs = [jnp.dot(x, wb_ref[...], preferred_element_type=F32) for wb_ref in wb_refs]
        o_ref[...] = epilogue(accs, [b[...] for b in b_refs], [e[...] for e in e_refs]).astype(o_ref.dtype)

    @pl.when(jnp.logical_not(valid))
    def _():
        o_ref[...] = jnp.zeros(o_ref.shape, o_ref.dtype)


def _gmm(x, w, col_offsets, bias, be, nu, *, tm, tn, n_out, out_dtype, epilogue, extras=(), name):
    r, k = x.shape
    nb = r // tm
    assert r % tm == 0 and n_out % tn == 0 and all(o % tn == 0 for o in col_offsets)
    n_w = len(col_offsets)

    def row(i, nu_ref):
        return jnp.minimum(i, nu_ref[0] - 1)

    in_specs = [pl.BlockSpec((tm, k), lambda j, i, be_r, nu_r: (row(i, nu_r), 0))]
    args = [x]
    for off in col_offsets:
        ob = off // tn
        in_specs.append(pl.BlockSpec((None, k, tn),
                                     lambda j, i, be_r, nu_r, ob=ob: (be_r[row(i, nu_r)], 0, j + ob)))
        args.append(w)
    n_b = 0
    if bias is not None:
        for off in col_offsets:
            ob = off // tn
            in_specs.append(pl.BlockSpec((None, 1, tn),
                                         lambda j, i, be_r, nu_r, ob=ob: (be_r[row(i, nu_r)], 0, j + ob)))
            args.append(bias)
            n_b += 1
    for arr, kind in extras:
        if kind == "tile":
            in_specs.append(pl.BlockSpec((tm, tn), lambda j, i, be_r, nu_r: (row(i, nu_r), j)))
        elif kind == "row":
            in_specs.append(pl.BlockSpec((tm, arr.shape[1]), lambda j, i, be_r, nu_r: (row(i, nu_r), 0)))
        else:
            in_specs.append(pl.BlockSpec((arr.shape[0], tn), lambda j, i, be_r, nu_r: (0, j)))
        args.append(arr)
    vmem = (2 * tm * k * 2 + n_w * (2 * k * tn * 4 + k * tn * 2) + n_w * tm * tn * 4
            + 2 * tm * tn * jnp.dtype(out_dtype).itemsize)
    for arr, kind in extras:
        blk = {"tile": tm * tn, "row": tm * arr.shape[1], "col": arr.shape[0] * tn}[kind]
        vmem += 2 * blk * arr.dtype.itemsize
    assert vmem <= VMEM_BUDGET_BYTES, (name, vmem)
    cast_rows = _tile(k, 256)
    body = functools.partial(_gmm_body, n_w=n_w, n_b=n_b, n_extra=len(extras), epilogue=epilogue,
                             cast_rows=cast_rows)
    return pl.pallas_call(
        body,
        grid_spec=pltpu.PrefetchScalarGridSpec(
            num_scalar_prefetch=2,
            grid=(n_out // tn, nb),
            in_specs=in_specs,
            out_specs=pl.BlockSpec((tm, tn), lambda j, i, be_r, nu_r: (i, j)),
            scratch_shapes=[pltpu.VMEM((k, tn), BF16) for _ in range(n_w)]),
        out_shape=jax.ShapeDtypeStruct((r, n_out), out_dtype),
        compiler_params=_params(("arbitrary", "arbitrary")),
        name=name,
    )(be, nu, *args)


def _dense(x, w2d, col_offsets, *, tm, tn, n_out, out_dtype, epilogue, extras=(), name):
    nb = x.shape[0] // tm
    return _gmm(x, w2d[None], col_offsets, None, jnp.zeros((nb,), jnp.int32),
                jnp.full((1,), nb, jnp.int32), tm=tm, tn=tn, n_out=n_out, out_dtype=out_dtype,
                epilogue=epilogue, extras=extras, name=name)


def _emm_body(ge_ref, gb_ref, ns_ref, nu_ref, x_hbm, *refs, n_w, n_b, epilogue, sb, tn, n_blocks,
              cast_rows):
    w_refs = refs[:n_w]
    b_refs = refs[n_w:n_w + n_b]
    o_hbm = refs[n_w + n_b]
    xbuf, stage, pend, xsem, osem = refs[n_w + n_b + 1:n_w + n_b + 6]
    wb_refs = refs[n_w + n_b + 6:]
    g = pl.program_id(0)
    n = pl.program_id(1)
    nsub = ns_ref[g]
    blk0 = gb_ref[g]
    k = x_hbm.shape[1]
    n_col = o_hbm.shape[1] // tn

    def x_copy(s):
        row = pl.multiple_of((blk0 + s) * sb, sb)
        return pltpu.make_async_copy(x_hbm.at[pl.ds(row, sb), :], xbuf.at[s], xsem)

    def out_copy(half, row, col):
        return pltpu.make_async_copy(stage.at[half],
                                     o_hbm.at[pl.ds(pl.multiple_of(row, sb), sb),
                                              pl.ds(pl.multiple_of(col, tn), tn)], osem.at[half])

    def wait_half(half):
        @pl.when(pend[3 * half] == 1)
        def _():
            out_copy(half, pend[3 * half + 1], pend[3 * half + 2]).wait()
            pend[3 * half] = 0

    @pl.when(jnp.logical_and(g == 0, n == 0))
    def _():
        pend[0] = 0
        pend[3] = 0

    @pl.when(n == 0)
    def _():
        def start(s, carry):
            x_copy(s).start()
            return carry
        lax.fori_loop(0, nsub, start, 0)

        def wait(s, carry):
            x_copy(s).wait()
            return carry
        lax.fori_loop(0, nsub, wait, 0)

    @pl.when(nsub > 0)
    def _():
        def cast(r, carry):
            sl = pl.ds(pl.multiple_of(r * cast_rows, cast_rows), cast_rows)
            for w_ref, wb_ref in zip(w_refs, wb_refs):
                wb_ref[sl, :] = w_ref[sl, :].astype(BF16)
            return carry
        lax.fori_loop(0, k // cast_rows, cast, 0)

    def sub(s, carry):
        half = s % 2
        x = xbuf[s]
        accs = [jnp.dot(x, wb_ref[...], preferred_element_type=F32) for wb_ref in wb_refs]
        res = epilogue(accs, [b[...] for b in b_refs], []).astype(stage.dtype)
        wait_half(half)
        stage[half] = res
        row = (blk0 + s) * sb
        col = n * tn
        out_copy(half, row, col).start()
        pend[3 * half] = 1
        pend[3 * half + 1] = row
        pend[3 * half + 2] = col
        return carry
    lax.fori_loop(0, nsub, sub, 0)

    @pl.when(jnp.logical_and(g == pl.num_programs(0) - 1, n == pl.num_programs(1) - 1))
    def _():
        wait_half(0)
        wait_half(1)
        stage[0] = jnp.zeros(stage.shape[1:], stage.dtype)

        def zero_start(blk, carry):
            for c in range(n_col):
                out_copy(0, blk * sb, c * tn).start()
            return carry
        lax.fori_loop(nu_ref[0], n_blocks, zero_start, 0)

        def zero_wait(blk, carry):
            for c in range(n_col):
                out_copy(0, blk * sb, c * tn).wait()
            return carry
        lax.fori_loop(nu_ref[0], n_blocks, zero_wait, 0)


def _emm(x, w, col_offsets, bias, groups, nu, *, sb, n_sub_max, tn, n_out, out_dtype, epilogue, name):
    r, k = x.shape
    ge, gb, ns = groups
    n_groups = ge.shape[0]
    n_col = n_out // tn
    n_w = len(col_offsets)
    assert r % sb == 0 and n_out % tn == 0 and all(o % tn == 0 for o in col_offsets)

    def wmap(ob):
        def index(g, n, ge_r, gb_r, ns_r, nu_r):
            return ge_r[g], 0, jnp.where(ns_r[g] > 0, n, n_col - 1) + ob
        return index

    in_specs = [pl.BlockSpec(memory_space=pl.ANY)]
    args = [x]
    for off in col_offsets:
        in_specs.append(pl.BlockSpec((None, k, tn), wmap(off // tn)))
        args.append(w)
    for off in col_offsets:
        in_specs.append(pl.BlockSpec((None, 1, tn), wmap(off // tn)))
        args.append(bias)
    out_bytes = jnp.dtype(out_dtype).itemsize
    vmem = (n_sub_max * sb * k * 2 + n_w * (2 * k * tn * 4 + k * tn * 2) + n_w * sb * tn * 4
            + 2 * sb * tn * out_bytes)
    assert vmem <= VMEM_BUDGET_BYTES, (name, vmem)
    body = functools.partial(_emm_body, n_w=n_w, n_b=n_w, epilogue=epilogue, sb=sb, tn=tn,
                             n_blocks=r // sb, cast_rows=_tile(k, 256))
    return pl.pallas_call(
        body,
        grid_spec=pltpu.PrefetchScalarGridSpec(
            num_scalar_prefetch=4,
            grid=(n_groups, n_col),
            in_specs=in_specs,
            out_specs=pl.BlockSpec(memory_space=pl.ANY),
            scratch_shapes=[pltpu.VMEM((n_sub_max, sb, k), BF16),
                            pltpu.VMEM((2, sb, tn), out_dtype),
                            pltpu.SMEM((8,), jnp.int32),
                            pltpu.SemaphoreType.DMA(()),
                            pltpu.SemaphoreType.DMA((2,))]
                           + [pltpu.VMEM((k, tn), BF16) for _ in range(n_w)]),
        out_shape=jax.ShapeDtypeStruct((r, n_out), out_dtype),
        compiler_params=_params(("arbitrary", "arbitrary")),
        name=name,
    )(ge, gb, ns, nu, *args)


def _expert_groups(counts, pad_starts, *, sb, n_sub_max, n_blocks):
    n_experts = counts.shape[0]
    nblk = (counts + sb - 1) // sb
    ngrp = (nblk + n_sub_max - 1) // n_sub_max
    gend = jnp.cumsum(ngrp)
    gstart = gend - ngrp
    n_groups = n_experts + -(-n_blocks // n_sub_max)
    gid = jnp.arange(n_groups, dtype=jnp.int32)
    valid = gid < gend[-1]
    last = jnp.maximum(gend[-1] - 1, 0)
    ge = jnp.sum(gend[None, :] <= jnp.where(valid, gid, last)[:, None], axis=1)
    ge = jnp.minimum(ge, n_experts - 1).astype(jnp.int32)
    j = gid - gstart[ge]
    gb = pad_starts[ge] // sb + j * n_sub_max
    ns = jnp.where(valid, jnp.clip(nblk[ge] - j * n_sub_max, 0, n_sub_max), 0)
    return ge, gb.astype(jnp.int32), ns.astype(jnp.int32)


def _sigmoid(x):
    return 1.0 / (1.0 + jnp.exp(-x))


def _ep_plain(accs, biases, extras):
    return accs[0]


def _ep_glu(accs, biases, extras):
    return accs[0] * _sigmoid(accs[1])


def _ep_residual(accs, biases, extras):
    return accs[0] + extras[0]


def _ep_swiglu(accs, biases, extras):
    g = jnp.minimum(accs[0] + biases[0], SWIGLU_LIMIT)
    up = jnp.clip(accs[1] + biases[1], -SWIGLU_LIMIT, SWIGLU_LIMIT)
    return g * _sigmoid(SWIGLU_ALPHA * g) * (up + 1.0)


def _ep_bias(accs, biases, extras):
    return accs[0] + biases[0]


def _ep_ple(accs, biases, extras):
    h, p, wp = extras
    pp = jnp.dot(p, wp.astype(BF16), preferred_element_type=F32)
    return h + _sigmoid(accs[0]) * pp


def _loga_body(x_ref, wa_ref, wg_ref, bg_ref, o_ref):
    a = jnp.dot(x_ref[...], wa_ref[...], preferred_element_type=F32)
    z = jnp.dot(a.astype(BF16), wg_ref[...], preferred_element_type=F32) + bg_ref[...]
    log_sig = jnp.minimum(z, 0.0) - jnp.log1p(jnp.exp(-jnp.abs(z)))
    o_ref[...] = log_sig / GATE_NORMALIZER


def _log_decay(xn, w_alr, w_gate_lr, b_gate):
    t, d = xn.shape
    rank, qk = w_gate_lr.shape
    assert rank <= LANES
    wa = jnp.zeros((d, LANES), BF16).at[:, :rank].set(w_alr.astype(BF16))
    wg = jnp.zeros((LANES, qk), BF16).at[:rank, :].set(w_gate_lr.astype(BF16))
    tm = _tile(t, 512)
    return pl.pallas_call(
        _loga_body,
        grid=(t // tm,),
        in_specs=[pl.BlockSpec((tm, d), lambda i: (i, 0)),
                  pl.BlockSpec((d, LANES), lambda i: (0, 0)),
                  pl.BlockSpec((LANES, qk), lambda i: (0, 0)),
                  pl.BlockSpec((1, qk), lambda i: (0, 0))],
        out_specs=pl.BlockSpec((tm, qk), lambda i: (i, 0)),
        out_shape=jax.ShapeDtypeStruct((t, qk), F32),
        compiler_params=_params(("arbitrary",)),
        name="gla_log_decay",
    )(xn, wa, wg, b_gate.reshape(1, qk))


def _split3(x):
    hi = x.astype(BF16)
    r1 = x - hi.astype(F32)
    mid = r1.astype(BF16)
    lo = (r1 - mid.astype(F32)).astype(BF16)
    return hi, mid, lo


def _gla_body(q_ref, k_ref, v_ref, r_ref, la_ref, s0_ref, gn_ref, og_ref, sn_ref, s_ref,
              *, chunk, n_sub, lc, scale):
    c = pl.program_id(1)

    @pl.when(c == 0)
    def _():
        s_ref[...] = s0_ref[...]

    nb, heads, dk, dv = s_ref.shape
    assert n_sub <= LANES
    rows = lax.broadcasted_iota(jnp.int32, (lc, lc), 0)
    cols = lax.broadcasted_iota(jnp.int32, (lc, lc), 1)
    same_chunk = (rows // chunk) == (cols // chunk)
    causal = jnp.logical_and(same_chunk, rows >= cols)
    tril = causal.astype(BF16)
    chunk_sum = same_chunk.astype(BF16)
    chunk_col = (lax.broadcasted_iota(jnp.int32, (lc, LANES), 0) // chunk
                 == lax.broadcasted_iota(jnp.int32, (lc, LANES), 1)).astype(BF16)
    tdims = (((0,), (0,)), ((), ()))

    chains = []
    for bb in range(nb):
        sl = slice(bb * lc, (bb + 1) * lc)
        for h in range(heads):
            kc = slice(h * dk, (h + 1) * dk)
            vc = slice(h * dv, (h + 1) * dv)
            q = q_ref[sl, kc] * scale
            k = k_ref[sl, kc]
            v = v_ref[sl, vc].astype(BF16)
            pieces = _split3(la_ref[sl, kc])
            b = sum(jnp.dot(tril, p, preferred_element_type=F32) for p in pieces)
            b_end = sum(jnp.dot(chunk_sum, p, preferred_element_type=F32) for p in pieces)
            b_end_col = sum(lax.dot_general(p, chunk_col, tdims, preferred_element_type=F32)
                            for p in pieces)
            decay = jnp.exp(b_end_col)
            qe = (q * jnp.exp(b)).astype(BF16)
            ke = (k * jnp.exp(-b)).astype(BF16)
            kd = (k * jnp.exp(b_end - b)).astype(BF16)
            att = lax.dot_general(qe, ke, (((1,), (1,)), ((), ())), preferred_element_type=F32)
            att = jnp.where(causal, att, 0.0).astype(BF16)
            o_intra = jnp.dot(att, v, preferred_element_type=F32)
            chains.append((bb, h, vc, qe, kd, v, decay, o_intra))

    for i in range(n_sub):
        cs = slice(i * chunk, (i + 1) * chunk)
        for bb, h, vc, qe, kd, v, decay, o_intra in chains:
            s = s_ref[bb, h]
            o = jnp.dot(qe[cs], s.astype(BF16), preferred_element_type=F32) + o_intra[cs]
            s_ref[bb, h] = (decay[:, i:i + 1] * s
                            + lax.dot_general(kd[cs], v[cs], tdims, preferred_element_type=F32))
            og_ref[bb * lc + i * chunk:bb * lc + (i + 1) * chunk, vc] = o

    o = og_ref[...]
    r = r_ref[...]
    for h in range(heads):
        vc = slice(h * dv, (h + 1) * dv)
        oh = o[:, vc]
        oh = oh * lax.rsqrt(jnp.mean(oh * oh, axis=-1, keepdims=True) + EPS) * gn_ref[:, vc]
        og_ref[:, vc] = (oh * (r[:, vc] * _sigmoid(r[:, vc]))).astype(og_ref.dtype)

    @pl.when(c == pl.num_programs(1) - 1)
    def _():
        sn_ref[...] = s_ref[...]


def _gla(qkvr, log_a, s0, gla_norm, *, row0, batch, length):
    _, heads, dk, dv = s0.shape
    qk, d_gla = heads * dk, heads * dv
    chunk = min(GLA_CHUNK, length)
    lc = _tile(length, 256, chunk)
    nc = length // lc
    nb = 2 if (nc == 1 and batch % 2 == 0) else 1
    rows = nb * lc
    assert length % chunk == 0 and row0 % rows == 0 and (2 * qk) % d_gla == 0
    rb0 = row0 // rows
    v_blk = 2 * qk // d_gla
    body = functools.partial(_gla_body, chunk=chunk, n_sub=lc // chunk, lc=lc, scale=dk ** -0.5)
    rowblk = lambda b, c: rb0 + b * nc + c
    og, s_new = pl.pallas_call(
        body,
        grid=(batch // nb, nc),
        in_specs=[pl.BlockSpec((rows, qk), lambda b, c: (rowblk(b, c), 0)),
                  pl.BlockSpec((rows, qk), lambda b, c: (rowblk(b, c), 1)),
                  pl.BlockSpec((rows, d_gla), lambda b, c: (rowblk(b, c), v_blk)),
                  pl.BlockSpec((rows, d_gla), lambda b, c: (rowblk(b, c), v_blk + 1)),
                  pl.BlockSpec((rows, qk), lambda b, c: (rowblk(b, c), 0)),
                  pl.BlockSpec((nb, heads, dk, dv), lambda b, c: (b, 0, 0, 0)),
                  pl.BlockSpec((1, d_gla), lambda b, c: (0, 0))],
        out_specs=[pl.BlockSpec((rows, d_gla), lambda b, c: (b * nc + c, 0)),
                   pl.BlockSpec((nb, heads, dk, dv), lambda b, c: (b, 0, 0, 0))],
        out_shape=[jax.ShapeDtypeStruct((batch * length, d_gla), F32),
                   jax.ShapeDtypeStruct((batch, heads, dk, dv), F32)],
        scratch_shapes=[pltpu.VMEM((nb, heads, dk, dv), F32)],
        compiler_params=_params(("arbitrary", "arbitrary")),
        name="gla",
    )(qkvr, qkvr, qkvr, qkvr, log_a, s0, gla_norm.reshape(1, d_gla))
    return og, s_new


def _conv_body(u_ref, st_ref, w_ref, b_ref, g_ref, beta_ref, c_ref, sn_ref, ext_ref, y_ref,
               *, width, lc, pad, row_tile, ch_tile):
    c = pl.program_id(1)
    hist = width - 1
    ch = u_ref.shape[1]

    @pl.when(c == 0)
    def _():
        if pad:
            ext_ref[0:pad, :] = jnp.zeros((pad, ch), F32)
        ext_ref[pad:pad + hist, :] = st_ref[...]

    base = pad + hist
    ext_ref[base:base + lc, :] = u_ref[...]

    def taps(ci, carry):
        cs = pl.ds(pl.multiple_of(ci * ch_tile, ch_tile), ch_tile)
        for r0 in range(0, lc, row_tile):
            acc = jnp.zeros((row_tile, ch_tile), F32) + b_ref[:, cs]
            for r in range(min(8, width)):
                a_max = (width - 1 - r) // 8
                start = pad + r + r0
                shifted = ext_ref[start:start + 8 * a_max + row_tile, cs]
                for a in range(a_max + 1):
                    j = 8 * a + r
                    acc = acc + w_ref[j:j + 1, cs] * shifted[8 * a:8 * a + row_tile]
            y_ref[r0:r0 + row_tile, cs] = acc
        return carry
    lax.fori_loop(0, ch // ch_tile, taps, 0)

    acc = y_ref[...]
    mu = jnp.mean(acc, axis=-1, keepdims=True)
    cen = acc - mu
    var = jnp.mean(cen * cen, axis=-1, keepdims=True)
    y = cen * lax.rsqrt(var + EPS) * g_ref[...] + beta_ref[...]
    c_ref[...] = (y * _sigmoid(y)).astype(c_ref.dtype)
    tail = ext_ref[pad + lc:pad + lc + hist, :]
    ext_ref[pad:pad + hist, :] = tail

    @pl.when(c == pl.num_programs(1) - 1)
    def _():
        sn_ref[...] = tail


def _conv(u, state, conv_w, conv_b, ln_g, ln_b, *, row0, batch, length):
    ch = u.shape[1]
    width = conv_w.shape[0]
    hist = width - 1
    lc = _tile(length, 128)
    nc = length // lc
    assert row0 % lc == 0 and lc % 8 == 0
    rb0 = row0 // lc
    pad = (-hist) % 8
    row_tile = _tile(lc, 64)
    body = functools.partial(_conv_body, width=width, lc=lc, pad=pad, row_tile=row_tile,
                             ch_tile=_tile(ch, 2 * LANES, LANES))
    vec = lambda a: a.reshape(1, ch)
    return pl.pallas_call(
        body,
        grid=(batch, nc),
        in_specs=[pl.BlockSpec((lc, ch), lambda b, c: (rb0 + b * nc + c, 0)),
                  pl.BlockSpec((None, hist, ch), lambda b, c: (b, 0, 0)),
                  pl.BlockSpec((width, ch), lambda b, c: (0, 0)),
                  pl.BlockSpec((1, ch), lambda b, c: (0, 0)),
                  pl.BlockSpec((1, ch), lambda b, c: (0, 0)),
                  pl.BlockSpec((1, ch), lambda b, c: (0, 0))],
        out_specs=[pl.BlockSpec((lc, ch), lambda b, c: (b * nc + c, 0)),
                   pl.BlockSpec((None, hist, ch), lambda b, c: (b, 0, 0))],
        out_shape=[jax.ShapeDtypeStruct((batch * length, ch), F32),
                   jax.ShapeDtypeStruct((batch, hist, ch), F32)],
        scratch_shapes=[pltpu.VMEM((pad + hist + lc, ch), F32), pltpu.VMEM((lc, ch), F32)],
        compiler_params=_params(("arbitrary", "arbitrary")),
        name="conformer_conv",
    )(u, state, conv_w, vec(conv_b), vec(ln_g), vec(ln_b))


HIGH_HALF = -(1 << 16)


def _pack_bf16_pairs(x):
    half = x.shape[1] // 2
    lo = lax.bitcast_convert_type(x[:, :half].astype(BF16).astype(F32), jnp.int32)
    hi = lax.bitcast_convert_type(x[:, half:].astype(BF16).astype(F32), jnp.int32)
    return lax.shift_right_logical(lo, 16) | (hi & HIGH_HALF)


def _unpack_bf16_pairs(w):
    lo = lax.bitcast_convert_type(lax.shift_left(w, 16), F32).astype(BF16)
    hi = lax.bitcast_convert_type(w & HIGH_HALF, F32).astype(BF16)
    return lo, hi


def _router_body(h_ref, g_ref, w_ref, b_ref, xn_ref, code_ref, gate_ref, cnt_ref, carry_ref,
                 *, n_experts):
    i = pl.program_id(0)

    @pl.when(i == 0)
    def _():
        carry_ref[...] = jnp.zeros(carry_ref.shape, F32)

    h = h_ref[...]
    xn = h * lax.rsqrt(jnp.mean(h * h, axis=-1, keepdims=True) + EPS) * g_ref[...]
    xn_ref[...] = _pack_bf16_pairs(xn)
    tm = h.shape[0]
    x_hi = xn.astype(BF16)
    x_lo = (xn - x_hi.astype(F32)).astype(BF16)
    w = w_ref[...]
    w_hi = w.astype(BF16)
    w_lo = (w - w_hi.astype(F32)).astype(BF16)
    logits = (jnp.dot(x_hi, w_hi, preferred_element_type=F32)
              + jnp.dot(x_lo, w_hi, preferred_element_type=F32)
              + jnp.dot(x_hi, w_lo, preferred_element_type=F32)) + b_ref[...]
    lane = lax.broadcasted_iota(jnp.int32, (tm, LANES), 1)
    lane_f = lane.astype(F32)
    work = jnp.where(lane < n_experts, logits, -jnp.inf)
    tops, hots = [], []
    for _ in range(TOP_K):
        m = jnp.max(work, axis=-1, keepdims=True)
        idx = jnp.min(jnp.where(work == m, lane_f, float(LANES)), axis=-1, keepdims=True)
        hot = lane_f == idx
        tops.append(m)
        hots.append(hot)
        work = jnp.where(hot, -jnp.inf, work)
    exps = [jnp.exp(t - tops[0]) for t in tops]
    denom = sum(exps)
    chosen = sum(hot.astype(F32) for hot in hots)
    rows = lax.broadcasted_iota(jnp.int32, (tm, tm), 0)
    cols = lax.broadcasted_iota(jnp.int32, (tm, tm), 1)
    before = (rows > cols).astype(BF16)
    seen = jnp.dot(before, chosen.astype(BF16), preferred_element_type=F32) + carry_ref[0:1, :]
    code = jnp.zeros((tm, LANES), jnp.int32)
    gate = jnp.zeros((tm, LANES), F32)
    for kk in range(TOP_K):
        hot_f = hots[kk].astype(F32)
        rank = jnp.sum(seen * hot_f, axis=-1, keepdims=True)
        expert = jnp.sum(lane_f * hot_f, axis=-1, keepdims=True)
        packed = expert.astype(jnp.int32) * (1 << RANK_BITS) + rank.astype(jnp.int32)
        code = jnp.where(lane == kk, packed, code)
        gate = jnp.where(lane == kk, exps[kk] / denom, gate)
    code_ref[...] = code
    gate_ref[...] = gate
    total = carry_ref[...] + jnp.sum(chosen, axis=0, keepdims=True)
    carry_ref[...] = total
    cnt_ref[...] = total


def _router(h, ln_g, w_router, b_router):
    t, d = h.shape
    n_experts = w_router.shape[1]
    assert n_experts <= LANES and t < (1 << RANK_BITS)
    wr = jnp.zeros((d, LANES), F32).at[:, :n_experts].set(w_router)
    br = jnp.zeros((1, LANES), F32).at[0, :n_experts].set(b_router)
    tm = _tile(t, 256)
    body = functools.partial(_router_body, n_experts=n_experts)
    return pl.pallas_call(
        body,
        grid=(t // tm,),
        in_specs=[pl.BlockSpec((tm, d), lambda i: (i, 0)),
                  pl.BlockSpec((1, d), lambda i: (0, 0)),
                  pl.BlockSpec((d, LANES), lambda i: (0, 0)),
                  pl.BlockSpec((1, LANES), lambda i: (0, 0))],
        out_specs=[pl.BlockSpec((tm, d // 2), lambda i: (i, 0)),
                   pl.BlockSpec((tm, LANES), lambda i: (i, 0)),
                   pl.BlockSpec((tm, LANES), lambda i: (i, 0)),
                   pl.BlockSpec((8, LANES), lambda i: (0, 0))],
        out_shape=[jax.ShapeDtypeStruct((t, d // 2), jnp.int32),
                   jax.ShapeDtypeStruct((t, LANES), jnp.int32),
                   jax.ShapeDtypeStruct((t, LANES), F32),
                   jax.ShapeDtypeStruct((8, LANES), F32)],
        scratch_shapes=[pltpu.VMEM((8, LANES), F32)],
        compiler_params=_params(("arbitrary",)),
        name="moe_router",
    )(h, ln_g.reshape(1, d), wr, br)


def _slot(code_ref, ps_ref, a):
    p = code_ref[a]
    return ps_ref[p >> RANK_BITS] + (p & ((1 << RANK_BITS) - 1))


def _dispatch_body(code_ref, ps_ref, cnt_ref, nu_ref, x_hbm, o_ref, tok_ref, buf_ref, sem,
                   *, n_tokens, n_experts, tm):
    i = pl.program_id(0)
    nu = nu_ref[0]

    def row_copy(blk, r):
        half = blk % 2
        return pltpu.make_async_copy(x_hbm.at[pl.ds(tok_ref[blk * tm + r], 1), :],
                                     buf_ref.at[half, pl.ds(r, 1), :], sem.at[half])

    def start_block(blk):
        def start(r, carry):
            row_copy(blk, r).start()
            return carry
        lax.fori_loop(0, tm, start, 0, unroll=8)

    @pl.when(i == 0)
    def _():
        def pad_expert(e, carry):
            first = ps_ref[e] + cnt_ref[e]
            last = ps_ref[e] + (cnt_ref[e] + tm - 1) // tm * tm

            def clear(s):
                tok_ref[s] = 0
                return s + 1
            lax.while_loop(lambda s: s < last, clear, first)
            return carry
        lax.fori_loop(0, n_experts, pad_expert, 0)

        def place(t):
            for kk in range(TOP_K):
                tok_ref[_slot(code_ref, ps_ref, t * TOP_K + kk)] = t
            return t + 1
        lax.while_loop(lambda t: t < n_tokens, place, 0)
        start_block(0)

    @pl.when(i + 1 < nu)
    def _():
        start_block(i + 1)

    @pl.when(i < nu)
    def _():
        def wait(r, carry):
            row_copy(i, r).wait()
            return carry
        lax.fori_loop(0, tm, wait, 0, unroll=8)
        half_d = buf_ref.shape[2]
        lo, hi = _unpack_bf16_pairs(buf_ref[i % 2])
        o_ref[:, :half_d] = lo
        o_ref[:, half_d:] = hi

    @pl.when(i >= nu)
    def _():
        o_ref[...] = jnp.zeros(o_ref.shape, o_ref.dtype)


def _dispatch(xn_packed, code, pad_starts, counts, nu, *, n_blocks, tm):
    t, half_d = xn_packed.shape
    d = 2 * half_d
    body = functools.partial(_dispatch_body, n_tokens=t, n_experts=counts.shape[0], tm=tm)
    return pl.pallas_call(
        body,
        grid_spec=pltpu.PrefetchScalarGridSpec(
            num_scalar_prefetch=4,
            grid=(n_blocks,),
            in_specs=[pl.BlockSpec(memory_space=pl.ANY)],
            out_specs=pl.BlockSpec((tm, d), lambda i, c_r, p_r, n_r, nu_r: (i, 0)),
            scratch_shapes=[pltpu.SMEM((n_blocks * tm,), jnp.int32),
                            pltpu.VMEM((2, tm, half_d), jnp.int32),
                            pltpu.SemaphoreType.DMA((2,))]),
        out_shape=jax.ShapeDtypeStruct((n_blocks * tm, d), BF16),
        compiler_params=_params(("arbitrary",)),
        name="moe_dispatch",
    )(code, pad_starts, counts, nu, xn_packed)


def _combine_body(code_ref, ps_ref, y_hbm, h_ref, gate_ref, g_ref, h_out_ref, xn_ref, buf_ref, sem, *, tm):
    i = pl.program_id(0)

    def row_copy(blk, r, kk):
        half = blk % 2
        slot = _slot(code_ref, ps_ref, (blk * tm + r) * TOP_K + kk)
        return pltpu.make_async_copy(y_hbm.at[pl.ds(slot, 1), :],
                                     buf_ref.at[half, kk, pl.ds(r, 1), :], sem.at[half])

    def start_block(blk):
        def start(r, carry):
            for kk in range(TOP_K):
                row_copy(blk, r, kk).start()
            return carry
        lax.fori_loop(0, tm, start, 0, unroll=2)

    @pl.when(i == 0)
    def _():
        start_block(0)

    @pl.when(i + 1 < pl.num_programs(0))
    def _():
        start_block(i + 1)

    def wait(r, carry):
        for kk in range(TOP_K):
            row_copy(i, r, kk).wait()
        return carry
    lax.fori_loop(0, tm, wait, 0, unroll=2)

    gate = gate_ref[...]
    h = h_ref[...]
    for kk in range(TOP_K):
        h = h + gate[:, kk:kk + 1] * buf_ref[i % 2, kk]
    h_out_ref[...] = h
    xn = h * lax.rsqrt(jnp.mean(h * h, axis=-1, keepdims=True) + EPS) * g_ref[...]
    xn_ref[...] = xn.astype(xn_ref.dtype)


def _combine(yb, h, gate, code, pad_starts, ln_g):
    t, d = h.shape
    tm = _tile(t, 128)
    body = functools.partial(_combine_body, tm=tm)
    return pl.pallas_call(
        body,
        grid_spec=pltpu.PrefetchScalarGridSpec(
            num_scalar_prefetch=2,
            grid=(t // tm,),
            in_specs=[pl.BlockSpec(memory_space=pl.ANY),
                      pl.BlockSpec((tm, d), lambda i, c_r, p_r: (i, 0)),
                      pl.BlockSpec((tm, LANES), lambda i, c_r, p_r: (i, 0)),
                      pl.BlockSpec((1, d), lambda i, c_r, p_r: (0, 0))],
            out_specs=[pl.BlockSpec((tm, d), lambda i, c_r, p_r: (i, 0)),
                       pl.BlockSpec((tm, d), lambda i, c_r, p_r: (i, 0))],
            scratch_shapes=[pltpu.VMEM((2, TOP_K, tm, d), F32),
                            pltpu.SemaphoreType.DMA((2,))]),
        out_shape=[jax.ShapeDtypeStruct((t, d), F32),
                   jax.ShapeDtypeStruct((t, d), BF16)],
        compiler_params=_params(("arbitrary",)),
        name="moe_combine",
    )(code, pad_starts, yb, h, gate, ln_g.reshape(1, d))


def _layer(x, p, groups, s_gla, s_conv, ln_mix, w_in, w_gate_lr, b_gate, gla_norm, conv_w, conv_b,
           conv_ln_g, conv_ln_b, w_out, ln_moe, w_router, b_router, w_gu, b_gu, w_dn, b_dn,
           ln_ple, w_ple_gate, w_ple_proj):
    t, d = x.shape
    heads, dk, dv = s_gla[0].shape[1:]
    qk, d_gla = heads * dk, heads * dv
    rank = w_gate_lr.shape[0]
    d_conv = conv_w.shape[1]
    n_experts, _, d_ff2 = w_gu.shape
    d_ff = d_ff2 // 2
    n_qkvr = 2 * qk + 2 * d_gla
    glu0 = n_qkvr + rank
    tm_dense = _tile(t, 1024)
    tn = 512

    xn1 = _rmsnorm(x, ln_mix, BF16)
    qkvr = _dense(xn1, w_in[:, :n_qkvr], [0], tm=tm_dense, tn=_tile(n_qkvr, tn, LANES), n_out=n_qkvr, out_dtype=F32,
                  epilogue=_ep_plain, name="in_proj_qkvr")
    u = _dense(xn1, w_in[:, glu0:], [0, d_conv], tm=tm_dense, tn=_tile(d_conv, tn // 2, LANES), n_out=d_conv,
               out_dtype=F32, epilogue=_ep_glu, name="in_proj_glu")
    log_a = _log_decay(xn1, w_in[:, n_qkvr:glu0], w_gate_lr, b_gate)
    og, cv, new_gla, new_conv = [], [], [], []
    for (row0, batch, length), sg, sc in zip(groups, s_gla, s_conv):
        o_g, s_g = _gla(qkvr, log_a, sg, gla_norm, row0=row0, batch=batch, length=length)
        o_c, s_c = _conv(u, sc, conv_w, conv_b, conv_ln_g, conv_ln_b, row0=row0, batch=batch, length=length)
        og.append(o_g)
        cv.append(o_c)
        new_gla.append(s_g)
        new_conv.append(s_c)
    mixed = jnp.concatenate([jnp.concatenate(og, 0), jnp.concatenate(cv, 0)], axis=1).astype(BF16)
    h1 = _dense(mixed, w_out, [0], tm=tm_dense, tn=_tile(d, tn, LANES), n_out=d, out_dtype=F32,
                epilogue=_ep_residual, extras=[(x, "tile")], name="out_proj")

    xn2, code, gate, counts = _router(h1, ln_moe, w_router, b_router)
    tm_e = _tile(t, EXPERT_ROWS)
    counts = counts[0, :n_experts].astype(jnp.int32)
    padded = (counts + tm_e - 1) // tm_e * tm_e
    pad_ends = jnp.cumsum(padded)
    pad_starts = (pad_ends - padded).astype(jnp.int32)
    n_blocks = -(-t * TOP_K // tm_e) + n_experts
    nu = (pad_ends[-1:] // tm_e).astype(jnp.int32)
    groups = _expert_groups(counts, pad_starts, sb=tm_e, n_sub_max=GROUP_BLOCKS, n_blocks=n_blocks)
    code = code[:, :TOP_K].reshape(-1)
    xb = _dispatch(xn2, code, pad_starts, counts, nu, n_blocks=n_blocks, tm=tm_e)
    act = _emm(xb, w_gu, [0, d_ff], b_gu.reshape(n_experts, 1, d_ff2), groups, nu, sb=tm_e,
               n_sub_max=GROUP_BLOCKS, tn=_tile(d_ff, tn, LANES), n_out=d_ff, out_dtype=BF16,
               epilogue=_ep_swiglu, name="moe_gate_up")
    yb = _emm(act, w_dn, [0], b_dn.reshape(n_experts, 1, d), groups, nu, sb=tm_e,
              n_sub_max=GROUP_BLOCKS, tn=_tile(d, 2 * tn, LANES), n_out=d, out_dtype=F32,
              epilogue=_ep_bias, name="moe_down")
    h2, xn3 = _combine(yb, h1, gate, code, pad_starts, ln_ple)

    h3 = _dense(xn3, w_ple_gate, [0], tm=_tile(t, 512), tn=_tile(d, tn, LANES), n_out=d, out_dtype=F32,
                epilogue=_ep_ple, extras=[(h2, "tile"), (p.astype(BF16), "row"), (w_ple_proj, "col")],
                name="ple")
    return h3, new_gla, new_conv


def kernel(x_prompt, x_sample, p_prompt, p_sample, state_gla, state_conv, ln_mix, w_in, w_gate_lr, b_gate, gla_norm, conv_w, conv_b, conv_ln_g, conv_ln_b, w_out, ln_moe, w_router, b_router, w_gu, b_gu, w_dn, b_dn, ln_ple, w_ple_gate, w_ple_proj, ln_final):
    bp, lp, d = x_prompt.shape
    bs, ls, _ = x_sample.shape
    depth = state_gla.shape[0]
    tp, ts = bp * lp, bs * ls
    groups = [(0, bp, lp), (tp, bs, ls)]
    h = jnp.concatenate([x_prompt.reshape(tp, d), x_sample.reshape(ts, d)], axis=0)
    zero_gla = jnp.zeros((bp,) + state_gla.shape[2:], state_gla.dtype)
    zero_conv = jnp.zeros((bp,) + state_conv.shape[2:], state_conv.dtype)
    gla_p, gla_s, conv_p, conv_s = [], [], [], []
    for i in range(depth):
        p = jnp.concatenate([p_prompt[i].reshape(tp, -1), p_sample[i].reshape(ts, -1)], axis=0)
        h, new_gla, new_conv = _layer(
            h, p, groups, [zero_gla, state_gla[i]], [zero_conv, state_conv[i]], ln_mix[i], w_in[i],
            w_gate_lr[i], b_gate[i], gla_norm[i], conv_w[i], conv_b[i], conv_ln_g[i], conv_ln_b[i],
            w_out[i], ln_moe[i], w_router[i], b_router[i], w_gu[i], b_gu[i], w_dn[i], b_dn[i],
            ln_ple[i], w_ple_gate[i], w_ple_proj[i])
        gla_p.append(new_gla[0])
        gla_s.append(new_gla[1])
        conv_p.append(new_conv[0])
        conv_s.append(new_conv[1])
    y_prompt = _rmsnorm(h, ln_final, F32, row0=0, rows=tp).reshape(bp, lp, d)
    y_sample = _rmsnorm(h, ln_final, F32, row0=tp, rows=ts).reshape(bs, ls, d)
    return (y_prompt, y_sample, jnp.stack(gla_p, 0), jnp.stack(conv_p, 0),
            jnp.stack(gla_s, 0), jnp.stack(conv_s, 0))
```

```python
import functools

import jax
import jax.numpy as jnp
from jax import lax
from jax.experimental import pallas as pl
from jax.experimental.pallas import tpu as pltpu

EPS = 1e-6
TOP_K = 4
GATE_NORMALIZER = 16.0
GLA_CHUNK = 16
SWIGLU_LIMIT = 7.0
SWIGLU_ALPHA = 1.702

LANES = 128
VMEM_LIMIT_BYTES = 60 * 2**20
VMEM_BUDGET_BYTES = 54 * 2**20
RANK_BITS = 16
EXPERT_ROWS = 256
GROUP_BLOCKS = 5

F32 = jnp.float32
BF16 = jnp.bfloat16


def _tile(n, pref, align=8):
    if n <= pref:
        return n
    for t in range(pref, 0, -1):
        if n % t == 0 and t % align == 0:
            return t
    return n


def _params(sem):
    return pltpu.CompilerParams(dimension_semantics=sem, vmem_limit_bytes=VMEM_LIMIT_BYTES)


def _rms_body(x_ref, g_ref, o_ref):
    x = x_ref[...]
    y = x * lax.rsqrt(jnp.mean(x * x, axis=-1, keepdims=True) + EPS)
    o_ref[...] = (y * g_ref[...]).astype(o_ref.dtype)


def _rmsnorm(x, g, out_dtype, row0=0, rows=None):
    rows = x.shape[0] if rows is None else rows
    d = x.shape[1]
    tm = _tile(rows, 512)
    assert row0 % tm == 0
    off = row0 // tm
    return pl.pallas_call(
        _rms_body,
        grid=(rows // tm,),
        in_specs=[pl.BlockSpec((tm, d), lambda i: (i + off, 0)),
                  pl.BlockSpec((1, d), lambda i: (0, 0))],
        out_specs=pl.BlockSpec((tm, d), lambda i: (i, 0)),
        out_shape=jax.ShapeDtypeStruct((rows, d), out_dtype),
        compiler_params=_params(("arbitrary",)),
        name="rmsnorm",
    )(x, g.reshape(1, d))


def _gmm_body(be_ref, nu_ref, *refs, n_w, n_b, n_extra, epilogue, cast_rows):
    x_ref = refs[0]
    w_refs = refs[1:1 + n_w]
    b_refs = refs[1 + n_w:1 + n_w + n_b]
    e_refs = refs[1 + n_w + n_b:1 + n_w + n_b + n_extra]
    o_ref = refs[1 + n_w + n_b + n_extra]
    wb_refs = refs[2 + n_w + n_b + n_extra:]
    i = pl.program_id(1)
    valid = i < nu_ref[0]
    prev = jnp.maximum(i - 1, 0)
    changed = jnp.logical_or(i == 0, be_ref[i] != be_ref[prev])
    k = w_refs[0].shape[0]

    @pl.when(jnp.logical_and(valid, changed))
    def _():
        def cast(r, carry):
            sl = pl.ds(pl.multiple_of(r * cast_rows, cast_rows), cast_rows)
            for w_ref, wb_ref in zip(w_refs, wb_refs):
                wb_ref[sl, :] = w_ref[sl, :].astype(BF16)
            return carry
        lax.fori_loop(0, k // cast_rows, cast, 0)

    @pl.when(valid)
    def _():
        x = x_ref[...]
        accs = [jnp.dot(x, wb_ref[...], preferred_element_type=F32) for wb_ref in wb_refs]
        o_ref[...] = epilogue(accs, [b[...] for b in b_refs], [e[...] for e in e_refs]).astype(o_ref.dtype)

    @pl.when(jnp.logical_not(valid))
    def _():
        o_ref[...] = jnp.zeros(o_ref.shape, o_ref.dtype)


def _gmm(x, w, col_offsets, bias, be, nu, *, tm, tn, n_out, out_dtype, epilogue, extras=(), name):
    r, k = x.shape
    nb = r // tm
    assert r % tm == 0 and n_out % tn == 0 and all(o % tn == 0 for o in col_offsets)
    n_w = len(col_offsets)

    def row(i, nu_ref):
        return jnp.minimum(i, nu_ref[0] - 1)

    in_specs = [pl.BlockSpec((tm, k), lambda j, i, be_r, nu_r: (row(i, nu_r), 0))]
    args = [x]
    for off in col_offsets:
        ob = off // tn
        in_specs.append(pl.BlockSpec((None, k, tn),
                                     lambda j, i, be_r, nu_r, ob=ob: (be_r[row(i, nu_r)], 0, j + ob)))
        args.append(w)
    n_b = 0
    if bias is not None:
        for off in col_offsets:
            ob = off // tn
            in_specs.append(pl.BlockSpec((None, 1, tn),
                                         lambda j, i, be_r, nu_r, ob=ob: (be_r[row(i, nu_r)], 0, j + ob)))
            args.append(bias)
            n_b += 1
    for arr, kind in extras:
        if kind == "tile":
            in_specs.append(pl.BlockSpec((tm, tn), lambda j, i, be_r, nu_r: (row(i, nu_r), j)))
        elif kind == "row":
            in_specs.append(pl.BlockSpec((tm, arr.shape[1]), lambda j, i, be_r, nu_r: (row(i, nu_r), 0)))
        else:
            in_specs.append(pl.BlockSpec((arr.shape[0], tn), lambda j, i, be_r, nu_r: (0, j)))
        args.append(arr)
    vmem = (2 * tm * k * 2 + n_w * (2 * k * tn * 4 + k * tn * 2) + n_w * tm * tn * 4
            + 2 * tm * tn * jnp.dtype(out_dtype).itemsize)
    for arr, kind in extras:
        blk = {"tile": tm * tn, "row": tm * arr.shape[1], "col": arr.shape[0] * tn}[kind]
        vmem += 2 * blk * arr.dtype.itemsize
    assert vmem <= VMEM_BUDGET_BYTES, (name, vmem)
    cast_rows = _tile(k, 256)
    body = functools.partial(_gmm_body, n_w=n_w, n_b=n_b, n_extra=len(extras), epilogue=epilogue,
                             cast_rows=cast_rows)
    return pl.pallas_call(
        body,
        grid_spec=pltpu.PrefetchScalarGridSpec(
            num_scalar_prefetch=2,
            grid=(n_out // tn, nb),
            in_specs=in_specs,
            out_specs=pl.BlockSpec((tm, tn), lambda j, i, be_r, nu_r: (i, j)),
            scratch_shapes=[pltpu.VMEM((k, tn), BF16) for _ in range(n_w)]),
        out_shape=jax.ShapeDtypeStruct((r, n_out), out_dtype),
        compiler_params=_params(("arbitrary", "arbitrary")),
        name=name,
    )(be, nu, *args)


def _dense(x, w2d, col_offsets, *, tm, tn, n_out, out_dtype, epilogue, extras=(), name):
    nb = x.shape[0] // tm
    return _gmm(x, w2d[None], col_offsets, None, jnp.zeros((nb,), jnp.int32),
                jnp.full((1,), nb, jnp.int32), tm=tm, tn=tn, n_out=n_out, out_dtype=out_dtype,
                epilogue=epilogue, extras=extras, name=name)


def _emm_body(ge_ref, gb_ref, ns_ref, nu_ref, x_hbm, *refs, n_w, n_b, epilogue, sb, tn, n_blocks,
              cast_rows):
    w_refs = refs[:n_w]
    b_refs = refs[n_w:n_w + n_b]
    o_hbm = refs[n_w + n_b]
    xbuf, stage, pend, xsem, osem = refs[n_w + n_b + 1:n_w + n_b + 6]
    wb_refs = refs[n_w + n_b + 6:]
    g = pl.program_id(0)
    n = pl.program_id(1)
    nsub = ns_ref[g]
    blk0 = gb_ref[g]
    k = x_hbm.shape[1]
    n_col = o_hbm.shape[1] // tn

    def x_copy(s):
        row = pl.multiple_of((blk0 + s) * sb, sb)
        return pltpu.make_async_copy(x_hbm.at[pl.ds(row, sb), :], xbuf.at[s], xsem)

    def out_copy(half, row, col):
        return pltpu.make_async_copy(stage.at[half],
                                     o_hbm.at[pl.ds(pl.multiple_of(row, sb), sb),
                                              pl.ds(pl.multiple_of(col, tn), tn)], osem.at[half])

    def wait_half(half):
        @pl.when(pend[3 * half] == 1)
        def _():
            out_copy(half, pend[3 * half + 1], pend[3 * half + 2]).wait()
            pend[3 * half] = 0

    @pl.when(jnp.logical_and(g == 0, n == 0))
    def _():
        pend[0] = 0
        pend[3] = 0

    @pl.when(n == 0)
    def _():
        def start(s, carry):
            x_copy(s).start()
            return carry
        lax.fori_loop(0, nsub, start, 0)

        def wait(s, carry):
            x_copy(s).wait()
            return carry
        lax.fori_loop(0, nsub, wait, 0)

    @pl.when(nsub > 0)
    def _():
        def cast(r, carry):
            sl = pl.ds(pl.multiple_of(r * cast_rows, cast_rows), cast_rows)
            for w_ref, wb_ref in zip(w_refs, wb_refs):
                wb_ref[sl, :] = w_ref[sl, :].astype(BF16)
            return carry
        lax.fori_loop(0, k // cast_rows, cast, 0)

    def sub(s, carry):
        half = s % 2
        x = xbuf[s]
        accs = [jnp.dot(x, wb_ref[...], preferred_element_type=F32) for wb_ref in wb_refs]
        res = epilogue(accs, [b[...] for b in b_refs], []).astype(stage.dtype)
        wait_half(half)
        stage[half] = res
        row = (blk0 + s) * sb
        col = n * tn
        out_copy(half, row, col).start()
        pend[3 * half] = 1
        pend[3 * half + 1] = row
        pend[3 * half + 2] = col
        return carry
    lax.fori_loop(0, nsub, sub, 0)

    @pl.when(jnp.logical_and(g == pl.num_programs(0) - 1, n == pl.num_programs(1) - 1))
    def _():
        wait_half(0)
        wait_half(1)
        stage[0] = jnp.zeros(stage.shape[1:], stage.dtype)

        def zero_start(blk, carry):
            for c in range(n_col):
                out_copy(0, blk * sb, c * tn).start()
            return carry
        lax.fori_loop(nu_ref[0], n_blocks, zero_start, 0)

        def zero_wait(blk, carry):
            for c in range(n_col):
                out_copy(0, blk * sb, c * tn).wait()
            return carry
        lax.fori_loop(nu_ref[0], n_blocks, zero_wait, 0)


def _emm(x, w, col_offsets, bias, groups, nu, *, sb, n_sub_max, tn, n_out, out_dtype, epilogue, name):
    r, k = x.shape
    ge, gb, ns = groups
    n_groups = ge.shape[0]
    n_col = n_out // tn
    n_w = len(col_offsets)
    assert r % sb == 0 and n_out % tn == 0 and all(o % tn == 0 for o in col_offsets)

    def wmap(ob):
        def index(g, n, ge_r, gb_r, ns_r, nu_r):
            return ge_r[g], 0, jnp.where(ns_r[g] > 0, n, n_col - 1) + ob
        return index

    in_specs = [pl.BlockSpec(memory_space=pl.ANY)]
    args = [x]
    for off in col_offsets:
        in_specs.append(pl.BlockSpec((None, k, tn), wmap(off // tn)))
        args.append(w)
    for off in col_offsets:
        in_specs.append(pl.BlockSpec((None, 1, tn), wmap(off // tn)))
        args.append(bias)
    out_bytes = jnp.dtype(out_dtype).itemsize
    vmem = (n_sub_max * sb * k * 2 + n_w * (2 * k * tn * 4 + k * tn * 2) + n_w * sb * tn * 4
            + 2 * sb * tn * out_bytes)
    assert vmem <= VMEM_BUDGET_BYTES, (name, vmem)
    body = functools.partial(_emm_body, n_w=n_w, n_b=n_w, epilogue=epilogue, sb=sb, tn=tn,
                             n_blocks=r // sb, cast_rows=_tile(k, 256))
    return pl.pallas_call(
        body,
        grid_spec=pltpu.PrefetchScalarGridSpec(
            num_scalar_prefetch=4,
            grid=(n_groups, n_col),
            in_specs=in_specs,
            out_specs=pl.BlockSpec(memory_space=pl.ANY),
            scratch_shapes=[pltpu.VMEM((n_sub_max, sb, k), BF16),
                            pltpu.VMEM((2, sb, tn), out_dtype),
                            pltpu.SMEM((8,), jnp.int32),
                            pltpu.SemaphoreType.DMA(()),
                            pltpu.SemaphoreType.DMA((2,))]
                           + [pltpu.VMEM((k, tn), BF16) for _ in range(n_w)]),
        out_shape=jax.ShapeDtypeStruct((r, n_out), out_dtype),
        compiler_params=_params(("arbitrary", "arbitrary")),
        name=name,
    )(ge, gb, ns, nu, *args)


def _expert_groups(counts, pad_starts, *, sb, n_sub_max, n_blocks):
    n_experts = counts.shape[0]
    nblk = (counts + sb - 1) // sb
    ngrp = (nblk + n_sub_max - 1) // n_sub_max
    gend = jnp.cumsum(ngrp)
    gstart = gend - ngrp
    n_groups = n_experts + -(-n_blocks // n_sub_max)
    gid = jnp.arange(n_groups, dtype=jnp.int32)
    valid = gid < gend[-1]
    last = jnp.maximum(gend[-1] - 1, 0)
    ge = jnp.sum(gend[None, :] <= jnp.where(valid, gid, last)[:, None], axis=1)
    ge = jnp.minimum(ge, n_experts - 1).astype(jnp.int32)
    j = gid - gstart[ge]
    gb = pad_starts[ge] // sb + j * n_sub_max
    ns = jnp.where(valid, jnp.clip(nblk[ge] - j * n_sub_max, 0, n_sub_max), 0)
    return ge, gb.astype(jnp.int32), ns.astype(jnp.int32)


def _sigmoid(x):
    return 1.0 / (1.0 + jnp.exp(-x))


def _ep_plain(accs, biases, extras):
    return accs[0]


def _ep_glu(accs, biases, extras):
    return accs[0] * _sigmoid(accs[1])


def _ep_residual(accs, biases, extras):
    return accs[0] + extras[0]


def _ep_swiglu(accs, biases, extras):
    g = jnp.minimum(accs[0] + biases[0], SWIGLU_LIMIT)
    up = jnp.clip(accs[1] + biases[1], -SWIGLU_LIMIT, SWIGLU_LIMIT)
    return g * _sigmoid(SWIGLU_ALPHA * g) * (up + 1.0)


def _ep_bias(accs, biases, extras):
    return accs[0] + biases[0]


def _ep_ple(accs, biases, extras):
    h, p, wp = extras
    pp = jnp.dot(p, wp.astype(BF16), preferred_element_type=F32)
    return h + _sigmoid(accs[0]) * pp


def _loga_body(x_ref, wa_ref, wg_ref, bg_ref, o_ref):
    a = jnp.dot(x_ref[...], wa_ref[...], preferred_element_type=F32)
    z = jnp.dot(a.astype(BF16), wg_ref[...], preferred_element_type=F32) + bg_ref[...]
    log_sig = jnp.minimum(z, 0.0) - jnp.log1p(jnp.exp(-jnp.abs(z)))
    o_ref[...] = log_sig / GATE_NORMALIZER


def _log_decay(xn, w_alr, w_gate_lr, b_gate):
    t, d = xn.shape
    rank, qk = w_gate_lr.shape
    assert rank <= LANES
    wa = jnp.zeros((d, LANES), BF16).at[:, :rank].set(w_alr.astype(BF16))
    wg = jnp.zeros((LANES, qk), BF16).at[:rank, :].set(w_gate_lr.astype(BF16))
    tm = _tile(t, 512)
    return pl.pallas_call(
        _loga_body,
        grid=(t // tm,),
        in_specs=[pl.BlockSpec((tm, d), lambda i: (i, 0)),
                  pl.BlockSpec((d, LANES), lambda i: (0, 0)),
                  pl.BlockSpec((LANES, qk), lambda i: (0, 0)),
                  pl.BlockSpec((1, qk), lambda i: (0, 0))],
        out_specs=pl.BlockSpec((tm, qk), lambda i: (i, 0)),
        out_shape=jax.ShapeDtypeStruct((t, qk), F32),
        compiler_params=_params(("arbitrary",)),
        name="gla_log_decay",
    )(xn, wa, wg, b_gate.reshape(1, qk))


def _split3(x):
    hi = x.astype(BF16)
    r1 = x - hi.astype(F32)
    mid = r1.astype(BF16)
    lo = (r1 - mid.astype(F32)).astype(BF16)
    return hi, mid, lo


def _gla_body(q_ref, k_ref, v_ref, r_ref, la_ref, s0_ref, gn_ref, mix_in_ref, og_ref, sn_ref, s_ref,
              o_ref, *, chunk, n_sub, lc, scale):
    del mix_in_ref
    c = pl.program_id(1)

    @pl.when(c == 0)
    def _():
        s_ref[...] = s0_ref[...]

    nb, heads, dk, dv = s_ref.shape
    assert n_sub <= LANES
    rows = lax.broadcasted_iota(jnp.int32, (lc, lc), 0)
    cols = lax.broadcasted_iota(jnp.int32, (lc, lc), 1)
    same_chunk = (rows // chunk) == (cols // chunk)
    causal = jnp.logical_and(same_chunk, rows >= cols)
    tril = causal.astype(BF16)
    chunk_sum = same_chunk.astype(BF16)
    chunk_col = (lax.broadcasted_iota(jnp.int32, (lc, LANES), 0) // chunk
                 == lax.broadcasted_iota(jnp.int32, (lc, LANES), 1)).astype(BF16)
    tdims = (((0,), (0,)), ((), ()))

    chains = []
    for bb in range(nb):
        sl = slice(bb * lc, (bb + 1) * lc)
        for h in range(heads):
            kc = slice(h * dk, (h + 1) * dk)
            vc = slice(h * dv, (h + 1) * dv)
            q = q_ref[sl, kc] * scale
            k = k_ref[sl, kc]
            v = v_ref[sl, vc].astype(BF16)
            pieces = _split3(la_ref[sl, kc])
            b = sum(jnp.dot(tril, p, preferred_element_type=F32) for p in pieces)
            b_end = sum(jnp.dot(chunk_sum, p, preferred_element_type=F32) for p in pieces)
            b_end_col = sum(lax.dot_general(p, chunk_col, tdims, preferred_element_type=F32)
                            for p in pieces)
            decay = jnp.exp(b_end_col)
            qe = (q * jnp.exp(b)).astype(BF16)
            ke = (k * jnp.exp(-b)).astype(BF16)
            kd = (k * jnp.exp(b_end - b)).astype(BF16)
            att = lax.dot_general(qe, ke, (((1,), (1,)), ((), ())), preferred_element_type=F32)
            att = jnp.where(causal, att, 0.0).astype(BF16)
            o_intra = jnp.dot(att, v, preferred_element_type=F32)
            chains.append((bb, h, vc, qe, kd, v, decay, o_intra))

    for i in range(n_sub):
        cs = slice(i * chunk, (i + 1) * chunk)
        for bb, h, vc, qe, kd, v, decay, o_intra in chains:
            s = s_ref[bb, h]
            o = jnp.dot(qe[cs], s.astype(BF16), preferred_element_type=F32) + o_intra[cs]
            s_ref[bb, h] = (decay[:, i:i + 1] * s
                            + lax.dot_general(kd[cs], v[cs], tdims, preferred_element_type=F32))
            o_ref[bb * lc + i * chunk:bb * lc + (i + 1) * chunk, vc] = o

    o = o_ref[...]
    r = r_ref[...]
    for h in range(heads):
        vc = slice(h * dv, (h + 1) * dv)
        oh = o[:, vc]
        oh = oh * lax.rsqrt(jnp.mean(oh * oh, axis=-1, keepdims=True) + EPS) * gn_ref[:, vc]
        og_ref[:, vc] = (oh * (r[:, vc] * _sigmoid(r[:, vc]))).astype(og_ref.dtype)

    @pl.when(c == pl.num_programs(1) - 1)
    def _():
        sn_ref[...] = s_ref[...]


def _gla(qkvr, log_a, s0, gla_norm, mixed, *, row0, batch, length):
    _, heads, dk, dv = s0.shape
    qk, d_gla = heads * dk, heads * dv
    chunk = min(GLA_CHUNK, length)
    lc = _tile(length, 256, chunk)
    nc = length // lc
    nb = 2 if (nc == 1 and batch % 2 == 0) else 1
    rows = nb * lc
    assert length % chunk == 0 and row0 % rows == 0 and (2 * qk) % d_gla == 0
    rb0 = row0 // rows
    v_blk = 2 * qk // d_gla
    body = functools.partial(_gla_body, chunk=chunk, n_sub=lc // chunk, lc=lc, scale=dk ** -0.5)
    rowblk = lambda b, c: rb0 + b * nc + c
    og, s_new = pl.pallas_call(
        body,
        grid=(batch // nb, nc),
        in_specs=[pl.BlockSpec((rows, qk), lambda b, c: (rowblk(b, c), 0)),
                  pl.BlockSpec((rows, qk), lambda b, c: (rowblk(b, c), 1)),
                  pl.BlockSpec((rows, d_gla), lambda b, c: (rowblk(b, c), v_blk)),
                  pl.BlockSpec((rows, d_gla), lambda b, c: (rowblk(b, c), v_blk + 1)),
                  pl.BlockSpec((rows, qk), lambda b, c: (rowblk(b, c), 0)),
                  pl.BlockSpec((nb, heads, dk, dv), lambda b, c: (b, 0, 0, 0)),
                  pl.BlockSpec((1, d_gla), lambda b, c: (0, 0)),
                  pl.BlockSpec(memory_space=pl.ANY)],
        out_specs=[pl.BlockSpec((rows, d_gla), lambda b, c: (rowblk(b, c), 0)),
                   pl.BlockSpec((nb, heads, dk, dv), lambda b, c: (b, 0, 0, 0))],
        out_shape=[jax.ShapeDtypeStruct(mixed.shape, mixed.dtype),
                   jax.ShapeDtypeStruct((batch, heads, dk, dv), F32)],
        scratch_shapes=[pltpu.VMEM((nb, heads, dk, dv), F32), pltpu.VMEM((rows, d_gla), F32)],
        input_output_aliases={7: 0},
        compiler_params=_params(("arbitrary", "arbitrary")),
        name="gla",
    )(qkvr, qkvr, qkvr, qkvr, log_a, s0, gla_norm.reshape(1, d_gla), mixed)
    return og, s_new


def _conv_body(u_ref, st_ref, w_ref, b_ref, g_ref, beta_ref, mix_in_ref, c_ref, sn_ref, ext_ref, y_ref,
               sh_ref, *, width, lc, pad, row_tile, ch_tile):
    del mix_in_ref
    c = pl.program_id(1)
    hist = width - 1
    nb = st_ref.shape[0]
    ch = u_ref.shape[1]

    @pl.when(c == 0)
    def _():
        for bb in range(nb):
            if pad:
                ext_ref[bb, 0:pad, :] = jnp.zeros((pad, ch), F32)
            ext_ref[bb, pad:pad + hist, :] = st_ref[bb]

    base = pad + hist
    for bb in range(nb):
        ext_ref[bb, base:base + lc, :] = u_ref[bb * lc:(bb + 1) * lc, :]

    def taps(ci, carry):
        cs = pl.ds(pl.multiple_of(ci * ch_tile, ch_tile), ch_tile)
        for bb in range(nb):
            for r0 in range(0, lc, row_tile):
                acc = jnp.zeros((row_tile, ch_tile), F32) + b_ref[:, cs]
                for r in range(min(8, width)):
                    a_max = (width - 1 - r) // 8
                    start = pad + r + r0
                    span = 8 * a_max + row_tile
                    sh_ref[0:span, :] = ext_ref[bb, start:start + span, cs]
                    for a in range(a_max + 1):
                        j = 8 * a + r
                        acc = acc + w_ref[j:j + 1, cs] * sh_ref[8 * a:8 * a + row_tile, :]
                y_ref[bb * lc + r0:bb * lc + r0 + row_tile, cs] = acc
        return carry
    lax.fori_loop(0, ch // ch_tile, taps, 0)

    acc = y_ref[...]
    mu = jnp.mean(acc, axis=-1, keepdims=True)
    cen = acc - mu
    var = jnp.mean(cen * cen, axis=-1, keepdims=True)
    y = cen * lax.rsqrt(var + EPS) * g_ref[...] + beta_ref[...]
    c_ref[...] = (y * _sigmoid(y)).astype(c_ref.dtype)
    for bb in range(nb):
        tail = ext_ref[bb, pad + lc:pad + lc + hist, :]
        ext_ref[bb, pad:pad + hist, :] = tail

    @pl.when(c == pl.num_programs(1) - 1)
    def _():
        for bb in range(nb):
            sn_ref[bb] = ext_ref[bb, pad:pad + hist, :]


def _conv(u, state, conv_w, conv_b, ln_g, ln_b, mixed, *, row0, batch, length):
    ch = u.shape[1]
    width = conv_w.shape[0]
    hist = width - 1
    lc = _tile(length, 128)
    nc = length // lc
    nb = 2 if (nc == 1 and batch % 2 == 0) else 1
    rows = nb * lc
    assert row0 % rows == 0 and lc % 8 == 0 and mixed.shape[1] == 2 * ch
    rb0 = row0 // rows
    pad = (-hist) % 8
    row_tile = _tile(lc, 64)
    ch_tile = _tile(ch, 2 * LANES, LANES)
    body = functools.partial(_conv_body, width=width, lc=lc, pad=pad, row_tile=row_tile, ch_tile=ch_tile)
    vec = lambda a: a.reshape(1, ch)
    return pl.pallas_call(
        body,
        grid=(batch // nb, nc),
        in_specs=[pl.BlockSpec((rows, ch), lambda b, c: (rb0 + b * nc + c, 0)),
                  pl.BlockSpec((nb, hist, ch), lambda b, c: (b, 0, 0)),
                  pl.BlockSpec((width, ch), lambda b, c: (0, 0)),
                  pl.BlockSpec((1, ch), lambda b, c: (0, 0)),
                  pl.BlockSpec((1, ch), lambda b, c: (0, 0)),
                  pl.BlockSpec((1, ch), lambda b, c: (0, 0)),
                  pl.BlockSpec(memory_space=pl.ANY)],
        out_specs=[pl.BlockSpec((rows, ch), lambda b, c: (rb0 + b * nc + c, 1)),
                   pl.BlockSpec((nb, hist, ch), lambda b, c: (b, 0, 0))],
        out_shape=[jax.ShapeDtypeStruct(mixed.shape, mixed.dtype),
                   jax.ShapeDtypeStruct((batch, hist, ch), F32)],
        scratch_shapes=[pltpu.VMEM((nb, pad + hist + lc, ch), F32), pltpu.VMEM((rows, ch), F32),
                        pltpu.VMEM((8 * ((width - 1) // 8) + row_tile, ch_tile), F32)],
        input_output_aliases={6: 0},
        compiler_params=_params(("arbitrary", "arbitrary")),
        name="conformer_conv",
    )(u, state, conv_w, vec(conv_b), vec(ln_g), vec(ln_b), mixed)


HIGH_HALF = -(1 << 16)


def _pack_bf16_pairs(x):
    half = x.shape[1] // 2
    lo = lax.bitcast_convert_type(x[:, :half].astype(BF16).astype(F32), jnp.int32)
    hi = lax.bitcast_convert_type(x[:, half:].astype(BF16).astype(F32), jnp.int32)
    return lax.shift_right_logical(lo, 16) | (hi & HIGH_HALF)


def _unpack_bf16_pairs(w):
    lo = lax.bitcast_convert_type(lax.shift_left(w, 16), F32).astype(BF16)
    hi = lax.bitcast_convert_type(w & HIGH_HALF, F32).astype(BF16)
    return lo, hi


def _router_body(h_ref, g_ref, w_ref, b_ref, xn_ref, code_ref, gate_ref, cnt_ref, carry_ref,
                 *, n_experts):
    i = pl.program_id(0)

    @pl.when(i == 0)
    def _():
        carry_ref[...] = jnp.zeros(carry_ref.shape, F32)

    h = h_ref[...]
    xn = h * lax.rsqrt(jnp.mean(h * h, axis=-1, keepdims=True) + EPS) * g_ref[...]
    xn_ref[...] = _pack_bf16_pairs(xn)
    tm = h.shape[0]
    x_hi = xn.astype(BF16)
    x_lo = (xn - x_hi.astype(F32)).astype(BF16)
    w = w_ref[...]
    w_hi = w.astype(BF16)
    w_lo = (w - w_hi.astype(F32)).astype(BF16)
    logits = (jnp.dot(x_hi, w_hi, preferred_element_type=F32)
              + jnp.dot(x_lo, w_hi, preferred_element_type=F32)
              + jnp.dot(x_hi, w_lo, preferred_element_type=F32)) + b_ref[...]
    lane = lax.broadcasted_iota(jnp.int32, (tm, LANES), 1)
    lane_f = lane.astype(F32)
    work = jnp.where(lane < n_experts, logits, -jnp.inf)
    tops, hots = [], []
    for _ in range(TOP_K):
        m = jnp.max(work, axis=-1, keepdims=True)
        idx = jnp.min(jnp.where(work == m, lane_f, float(LANES)), axis=-1, keepdims=True)
        hot = lane_f == idx
        tops.append(m)
        hots.append(hot)
        work = jnp.where(hot, -jnp.inf, work)
    exps = [jnp.exp(t - tops[0]) for t in tops]
    denom = sum(exps)
    chosen = sum(hot.astype(F32) for hot in hots)
    rows = lax.broadcasted_iota(jnp.int32, (tm, tm), 0)
    cols = lax.broadcasted_iota(jnp.int32, (tm, tm), 1)
    before = (rows > cols).astype(BF16)
    seen = jnp.dot(before, chosen.astype(BF16), preferred_element_type=F32) + carry_ref[0:1, :]
    code = jnp.zeros((tm, LANES), jnp.int32)
    gate = jnp.zeros((tm, LANES), F32)
    for kk in range(TOP_K):
        hot_f = hots[kk].astype(F32)
        rank = jnp.sum(seen * hot_f, axis=-1, keepdims=True)
        expert = jnp.sum(lane_f * hot_f, axis=-1, keepdims=True)
        packed = expert.astype(jnp.int32) * (1 << RANK_BITS) + rank.astype(jnp.int32)
        code = jnp.where(lane == kk, packed, code)
        gate = jnp.where(lane == kk, exps[kk] / denom, gate)
    code_ref[...] = code
    gate_ref[...] = gate
    total = carry_ref[...] + jnp.sum(chosen, axis=0, keepdims=True)
    carry_ref[...] = total
    cnt_ref[...] = total


def _router(h, ln_g, w_router, b_router):
    t, d = h.shape
    n_experts = w_router.shape[1]
    assert n_experts <= LANES and t < (1 << RANK_BITS)
    wr = jnp.zeros((d, LANES), F32).at[:, :n_experts].set(w_router)
    br = jnp.zeros((1, LANES), F32).at[0, :n_experts].set(b_router)
    tm = _tile(t, 256)
    body = functools.partial(_router_body, n_experts=n_experts)
    return pl.pallas_call(
        body,
        grid=(t // tm,),
        in_specs=[pl.BlockSpec((tm, d), lambda i: (i, 0)),
                  pl.BlockSpec((1, d), lambda i: (0, 0)),
                  pl.BlockSpec((d, LANES), lambda i: (0, 0)),
                  pl.BlockSpec((1, LANES), lambda i: (0, 0))],
        out_specs=[pl.BlockSpec((tm, d // 2), lambda i: (i, 0)),
                   pl.BlockSpec((tm, LANES), lambda i: (i, 0)),
                   pl.BlockSpec((tm, LANES), lambda i: (i, 0)),
                   pl.BlockSpec((8, LANES), lambda i: (0, 0))],
        out_shape=[jax.ShapeDtypeStruct((t, d // 2), jnp.int32),
                   jax.ShapeDtypeStruct((t, LANES), jnp.int32),
                   jax.ShapeDtypeStruct((t, LANES), F32),
                   jax.ShapeDtypeStruct((8, LANES), F32)],
        scratch_shapes=[pltpu.VMEM((8, LANES), F32)],
        compiler_params=_params(("arbitrary",)),
        name="moe_router",
    )(h, ln_g.reshape(1, d), wr, br)


def _slot(code_ref, ps_ref, a):
    p = code_ref[a]
    return ps_ref[p >> RANK_BITS] + (p & ((1 << RANK_BITS) - 1))


def _dispatch_body(code_ref, ps_ref, cnt_ref, nu_ref, x_hbm, o_ref, tok_ref, buf_ref, sem,
                   *, n_tokens, n_experts, tm):
    i = pl.program_id(0)
    nu = nu_ref[0]

    def row_copy(blk, j):
        half = blk % 2
        r = (j % (tm // 8)) * 8 + j // (tm // 8)
        return pltpu.make_async_copy(x_hbm.at[pl.ds(tok_ref[blk * tm + r], 1), :],
                                     buf_ref.at[half, pl.ds(r, 1), :], sem.at[half])

    def start_block(blk):
        def start(r, carry):
            row_copy(blk, r).start()
            return carry
        lax.fori_loop(0, tm, start, 0, unroll=8)

    @pl.when(i == 0)
    def _():
        def pad_expert(e, carry):
            first = ps_ref[e] + cnt_ref[e]
            last = ps_ref[e] + (cnt_ref[e] + tm - 1) // tm * tm

            def clear(s):
                tok_ref[s] = 0
                return s + 1
            lax.while_loop(lambda s: s < last, clear, first)
            return carry
        lax.fori_loop(0, n_experts, pad_expert, 0)

        def place(t):
            for kk in range(TOP_K):
                tok_ref[_slot(code_ref, ps_ref, t * TOP_K + kk)] = t
            return t + 1
        lax.while_loop(lambda t: t < n_tokens, place, 0)
        start_block(0)

    @pl.when(i + 1 < nu)
    def _():
        start_block(i + 1)

    @pl.when(i < nu)
    def _():
        def wait(r, carry):
            row_copy(i, r).wait()
            return carry
        lax.fori_loop(0, tm, wait, 0, unroll=8)
        half_d = buf_ref.shape[2]
        lo, hi = _unpack_bf16_pairs(buf_ref[i % 2])
        o_ref[:, :half_d] = lo
        o_ref[:, half_d:] = hi

    @pl.when(i >= nu)
    def _():
        o_ref[...] = jnp.zeros(o_ref.shape, o_ref.dtype)


def _dispatch(xn_packed, code, pad_starts, counts, nu, *, n_blocks, tm):
    t, half_d = xn_packed.shape
    d = 2 * half_d
    body = functools.partial(_dispatch_body, n_tokens=t, n_experts=counts.shape[0], tm=tm)
    return pl.pallas_call(
        body,
        grid_spec=pltpu.PrefetchScalarGridSpec(
            num_scalar_prefetch=4,
            grid=(n_blocks,),
            in_specs=[pl.BlockSpec(memory_space=pl.ANY)],
            out_specs=pl.BlockSpec((tm, d), lambda i, c_r, p_r, n_r, nu_r: (i, 0)),
            scratch_shapes=[pltpu.SMEM((n_blocks * tm,), jnp.int32),
                            pltpu.VMEM((2, tm, half_d), jnp.int32),
                            pltpu.SemaphoreType.DMA((2,))]),
        out_shape=jax.ShapeDtypeStruct((n_blocks * tm, d), BF16),
        compiler_params=_params(("arbitrary",)),
        name="moe_dispatch",
    )(code, pad_starts, counts, nu, xn_packed)


def _combine_body(code_ref, ps_ref, y_hbm, h_ref, gate_ref, g_ref, h_out_ref, xn_ref, buf_ref, sem, *, tm):
    i = pl.program_id(0)

    def row_copy(blk, r, kk):
        half = blk % 2
        slot = _slot(code_ref, ps_ref, (blk * tm + r) * TOP_K + kk)
        return pltpu.make_async_copy(y_hbm.at[pl.ds(slot, 1), :],
                                     buf_ref.at[half, kk, pl.ds(r, 1), :], sem.at[half])

    def start_block(blk):
        def start(r, carry):
            for kk in range(TOP_K):
                row_copy(blk, r, kk).start()
            return carry
        lax.fori_loop(0, tm, start, 0, unroll=2)

    @pl.when(i == 0)
    def _():
        start_block(0)

    @pl.when(i + 1 < pl.num_programs(0))
    def _():
        start_block(i + 1)

    def wait(r, carry):
        for kk in range(TOP_K):
            row_copy(i, r, kk).wait()
        return carry
    lax.fori_loop(0, tm, wait, 0, unroll=2)

    gate = gate_ref[...]
    h = h_ref[...]
    for kk in range(TOP_K):
        h = h + gate[:, kk:kk + 1] * buf_ref[i % 2, kk]
    h_out_ref[...] = h
    xn = h * lax.rsqrt(jnp.mean(h * h, axis=-1, keepdims=True) + EPS) * g_ref[...]
    xn_ref[...] = xn.astype(xn_ref.dtype)


def _combine(yb, h, gate, code, pad_starts, ln_g):
    t, d = h.shape
    tm = _tile(t, 128)
    body = functools.partial(_combine_body, tm=tm)
    return pl.pallas_call(
        body,
        grid_spec=pltpu.PrefetchScalarGridSpec(
            num_scalar_prefetch=2,
            grid=(t // tm,),
            in_specs=[pl.BlockSpec(memory_space=pl.ANY),
                      pl.BlockSpec((tm, d), lambda i, c_r, p_r: (i, 0)),
                      pl.BlockSpec((tm, LANES), lambda i, c_r, p_r: (i, 0)),
                      pl.BlockSpec((1, d), lambda i, c_r, p_r: (0, 0))],
            out_specs=[pl.BlockSpec((tm, d), lambda i, c_r, p_r: (i, 0)),
                       pl.BlockSpec((tm, d), lambda i, c_r, p_r: (i, 0))],
            scratch_shapes=[pltpu.VMEM((2, TOP_K, tm, d), F32),
                            pltpu.SemaphoreType.DMA((2,))]),
        out_shape=[jax.ShapeDtypeStruct((t, d), F32),
                   jax.ShapeDtypeStruct((t, d), BF16)],
        compiler_params=_params(("arbitrary",)),
        name="moe_combine",
    )(code, pad_starts, yb, h, gate, ln_g.reshape(1, d))


def _layer(x, p, groups, s_gla, s_conv, ln_mix, w_in, w_gate_lr, b_gate, gla_norm, conv_w, conv_b,
           conv_ln_g, conv_ln_b, w_out, ln_moe, w_router, b_router, w_gu, b_gu, w_dn, b_dn,
           ln_ple, w_ple_gate, w_ple_proj):
    t, d = x.shape
    heads, dk, dv = s_gla[0].shape[1:]
    qk, d_gla = heads * dk, heads * dv
    rank = w_gate_lr.shape[0]
    d_conv = conv_w.shape[1]
    n_experts, _, d_ff2 = w_gu.shape
    d_ff = d_ff2 // 2
    n_qkvr = 2 * qk + 2 * d_gla
    glu0 = n_qkvr + rank
    tm_dense = _tile(t, 1024)
    tn = 512

    xn1 = _rmsnorm(x, ln_mix, BF16)
    qkvr = _dense(xn1, w_in[:, :n_qkvr], [0], tm=tm_dense, tn=_tile(n_qkvr, tn, LANES), n_out=n_qkvr, out_dtype=F32,
                  epilogue=_ep_plain, name="in_proj_qkvr")
    u = _dense(xn1, w_in[:, glu0:], [0, d_conv], tm=tm_dense, tn=_tile(d_conv, tn // 2, LANES), n_out=d_conv,
               out_dtype=F32, epilogue=_ep_glu, name="in_proj_glu")
    log_a = _log_decay(xn1, w_in[:, n_qkvr:glu0], w_gate_lr, b_gate)
    new_gla, new_conv = [], []
    assert d_gla == d_conv
    mixed = jnp.zeros((t, d_gla + d_conv), BF16)
    for (row0, batch, length), sg, sc in zip(groups, s_gla, s_conv):
        mixed, s_g = _gla(qkvr, log_a, sg, gla_norm, mixed, row0=row0, batch=batch, length=length)
        mixed, s_c = _conv(u, sc, conv_w, conv_b, conv_ln_g, conv_ln_b, mixed, row0=row0, batch=batch,
                           length=length)
        new_gla.append(s_g)
        new_conv.append(s_c)
    h1 = _dense(mixed, w_out, [0], tm=tm_dense, tn=_tile(d, tn, LANES), n_out=d, out_dtype=F32,
                epilogue=_ep_residual, extras=[(x, "tile")], name="out_proj")

    xn2, code, gate, counts = _router(h1, ln_moe, w_router, b_router)
    tm_e = _tile(t, EXPERT_ROWS)
    counts = counts[0, :n_experts].astype(jnp.int32)
    padded = (counts + tm_e - 1) // tm_e * tm_e
    pad_ends = jnp.cumsum(padded)
    pad_starts = (pad_ends - padded).astype(jnp.int32)
    n_blocks = -(-t * TOP_K // tm_e) + n_experts
    nu = (pad_ends[-1:] // tm_e).astype(jnp.int32)
    groups = _expert_groups(counts, pad_starts, sb=tm_e, n_sub_max=GROUP_BLOCKS, n_blocks=n_blocks)
    code = code[:, :TOP_K].reshape(-1)
    xb = _dispatch(xn2, code, pad_starts, counts, nu, n_blocks=n_blocks, tm=tm_e)
    act = _emm(xb, w_gu, [0, d_ff], b_gu.reshape(n_experts, 1, d_ff2), groups, nu, sb=tm_e,
               n_sub_max=GROUP_BLOCKS, tn=_tile(d_ff, tn, LANES), n_out=d_ff, out_dtype=BF16,
               epilogue=_ep_swiglu, name="moe_gate_up")
    yb = _emm(act, w_dn, [0], b_dn.reshape(n_experts, 1, d), groups, nu, sb=tm_e,
              n_sub_max=GROUP_BLOCKS, tn=_tile(d, 2 * tn, LANES), n_out=d, out_dtype=F32,
              epilogue=_ep_bias, name="moe_down")
    h2, xn3 = _combine(yb, h1, gate, code, pad_starts, ln_ple)

    h3 = _dense(xn3, w_ple_gate, [0], tm=_tile(t, 512), tn=_tile(d, tn, LANES), n_out=d, out_dtype=F32,
                epilogue=_ep_ple, extras=[(h2, "tile"), (p.astype(BF16), "row"), (w_ple_proj, "col")],
                name="ple")
    return h3, new_gla, new_conv


def kernel(x_prompt, x_sample, p_prompt, p_sample, state_gla, state_conv, ln_mix, w_in, w_gate_lr, b_gate, gla_norm, conv_w, conv_b, conv_ln_g, conv_ln_b, w_out, ln_moe, w_router, b_router, w_gu, b_gu, w_dn, b_dn, ln_ple, w_ple_gate, w_ple_proj, ln_final):
    bp, lp, d = x_prompt.shape
    bs, ls, _ = x_sample.shape
    depth = state_gla.shape[0]
    tp, ts = bp * lp, bs * ls
    groups = [(0, bp, lp), (tp, bs, ls)]
    h = jnp.concatenate([x_prompt.reshape(tp, d), x_sample.reshape(ts, d)], axis=0)
    zero_gla = jnp.zeros((bp,) + state_gla.shape[2:], state_gla.dtype)
    zero_conv = jnp.zeros((bp,) + state_conv.shape[2:], state_conv.dtype)
    gla_p, gla_s, conv_p, conv_s = [], [], [], []
    for i in range(depth):
        p = jnp.concatenate([p_prompt[i].reshape(tp, -1), p_sample[i].reshape(ts, -1)], axis=0)
        h, new_gla, new_conv = _layer(
            h, p, groups, [zero_gla, state_gla[i]], [zero_conv, state_conv[i]], ln_mix[i], w_in[i],
            w_gate_lr[i], b_gate[i], gla_norm[i], conv_w[i], conv_b[i], conv_ln_g[i], conv_ln_b[i],
            w_out[i], ln_moe[i], w_router[i], b_router[i], w_gu[i], b_gu[i], w_dn[i], b_dn[i],
            ln_ple[i], w_ple_gate[i], w_ple_proj[i])
        gla_p.append(new_gla[0])
        gla_s.append(new_gla[1])
        conv_p.append(new_conv[0])
        conv_s.append(new_conv[1])
    y_prompt = _rmsnorm(h, ln_final, F32, row0=0, rows=tp).reshape(bp, lp, d)
    y_sample = _rmsnorm(h, ln_final, F32, row0=tp, rows=ts).reshape(bs, ls, d)
    return (y_prompt, y_sample, jnp.stack(gla_p, 0), jnp.stack(conv_p, 0),
            jnp.stack(gla_s, 0), jnp.stack(conv_s, 0))
```

```python
import functools

import jax
import jax.numpy as jnp
from jax import lax
from jax.experimental import pallas as pl
from jax.experimental.pallas import tpu as pltpu

EPS = 1e-6
TOP_K = 4
GATE_NORMALIZER = 16.0
GLA_CHUNK = 16
SWIGLU_LIMIT = 7.0
SWIGLU_ALPHA = 1.702

LANES = 128
VMEM_LIMIT_BYTES = 60 * 2**20
VMEM_BUDGET_BYTES = 54 * 2**20
RANK_BITS = 16
EXPERT_ROWS = 256
GROUP_BLOCKS = 5

F32 = jnp.float32
BF16 = jnp.bfloat16


def _tile(n, pref, align=8):
    if n <= pref:
        return n
    for t in range(pref, 0, -1):
        if n % t == 0 and t % align == 0:
            return t
    return n


def _params(sem):
    return pltpu.CompilerParams(dimension_semantics=sem, vmem_limit_bytes=VMEM_LIMIT_BYTES)


def _rms_body(x_ref, g_ref, o_ref):
    x = x_ref[...]
    y = x * lax.rsqrt(jnp.mean(x * x, axis=-1, keepdims=True) + EPS)
    o_ref[...] = (y * g_ref[...]).astype(o_ref.dtype)


def _rmsnorm(x, g, out_dtype, row0=0, rows=None):
    rows = x.shape[0] if rows is None else rows
    d = x.shape[1]
    tm = _tile(rows, 512)
    assert row0 % tm == 0
    off = row0 // tm
    return pl.pallas_call(
        _rms_body,
        grid=(rows // tm,),
        in_specs=[pl.BlockSpec((tm, d), lambda i: (i + off, 0)),
                  pl.BlockSpec((1, d), lambda i: (0, 0))],
        out_specs=pl.BlockSpec((tm, d), lambda i: (i, 0)),
        out_shape=jax.ShapeDtypeStruct((rows, d), out_dtype),
        compiler_params=_params(("arbitrary",)),
        name="rmsnorm",
    )(x, g.reshape(1, d))


def _gmm_body(be_ref, nu_ref, *refs, n_w, n_b, n_extra, epilogue, cast_rows):
    x_ref = refs[0]
    w_refs = refs[1:1 + n_w]
    b_refs = refs[1 + n_w:1 + n_w + n_b]
    e_refs = refs[1 + n_w + n_b:1 + n_w + n_b + n_extra]
    o_ref = refs[1 + n_w + n_b + n_extra]
    wb_refs = refs[2 + n_w + n_b + n_extra:]
    i = pl.program_id(1)
    valid = i < nu_ref[0]
    prev = jnp.maximum(i - 1, 0)
    changed = jnp.logical_or(i == 0, be_ref[i] != be_ref[prev])
    k = w_refs[0].shape[0]

    @pl.when(jnp.logical_and(valid, changed))
    def _():
        def cast(r, carry):
            sl = pl.ds(pl.multiple_of(r * cast_rows, cast_rows), cast_rows)
            for w_ref, wb_ref in zip(w_refs, wb_refs):
                wb_ref[sl, :] = w_ref[sl, :].astype(BF16)
            return carry
        lax.fori_loop(0, k // cast_rows, cast, 0)

    @pl.when(valid)
    def _():
        x = x_ref[...]
        accs = [jnp.dot(x, wb_ref[...], preferred_element_type=F32) for wb_ref in wb_refs]
        o_ref[...] = epilogue(accs, [b[...] for b in b_refs], [e[...] for e in e_refs]).astype(o_ref.dtype)

    @pl.when(jnp.logical_not(valid))
    def _():
        o_ref[...] = jnp.zeros(o_ref.shape, o_ref.dtype)


def _gmm(x, w, col_offsets, bias, be, nu, *, tm, tn, n_out, out_dtype, epilogue, extras=(), name):
    r, k = x.shape
    nb = r // tm
    assert r % tm == 0 and n_out % tn == 0 and all(o % tn == 0 for o in col_offsets)
    n_w = len(col_offsets)

    def row(i, nu_ref):
        return jnp.minimum(i, nu_ref[0] - 1)

    in_specs = [pl.BlockSpec((tm, k), lambda j, i, be_r, nu_r: (row(i, nu_r), 0))]
    args = [x]
    for off in col_offsets:
        ob = off // tn
        in_specs.append(pl.BlockSpec((None, k, tn),
                                     lambda j, i, be_r, nu_r, ob=ob: (be_r[row(i, nu_r)], 0, j + ob)))
        args.append(w)
    n_b = 0
    if bias is not None:
        for off in col_offsets:
            ob = off // tn
            in_specs.append(pl.BlockSpec((None, 1, tn),
                                         lambda j, i, be_r, nu_r, ob=ob: (be_r[row(i, nu_r)], 0, j + ob)))
            args.append(bias)
            n_b += 1
    for arr, kind in extras:
        if kind == "tile":
            in_specs.append(pl.BlockSpec((tm, tn), lambda j, i, be_r, nu_r: (row(i, nu_r), j)))
        elif kind == "row":
            in_specs.append(pl.BlockSpec((tm, arr.shape[1]), lambda j, i, be_r, nu_r: (row(i, nu_r), 0)))
        else:
            in_specs.append(pl.BlockSpec((arr.shape[0], tn), lambda j, i, be_r, nu_r: (0, j)))
        args.append(arr)
    vmem = (2 * tm * k * 2 + n_w * (2 * k * tn * 4 + k * tn * 2) + n_w * tm * tn * 4
            + 2 * tm * tn * jnp.dtype(out_dtype).itemsize)
    for arr, kind in extras:
        blk = {"tile": tm * tn, "row": tm * arr.shape[1], "col": arr.shape[0] * tn}[kind]
        vmem += 2 * blk * arr.dtype.itemsize
    assert vmem <= VMEM_BUDGET_BYTES, (name, vmem)
    cast_rows = _tile(k, 256)
    body = functools.partial(_gmm_body, n_w=n_w, n_b=n_b, n_extra=len(extras), epilogue=epilogue,
                             cast_rows=cast_rows)
    return pl.pallas_call(
        body,
        grid_spec=pltpu.PrefetchScalarGridSpec(
            num_scalar_prefetch=2,
            grid=(n_out // tn, nb),
            in_specs=in_specs,
            out_specs=pl.BlockSpec((tm, tn), lambda j, i, be_r, nu_r: (i, j)),
            scratch_shapes=[pltpu.VMEM((k, tn), BF16) for _ in range(n_w)]),
        out_shape=jax.ShapeDtypeStruct((r, n_out), out_dtype),
        compiler_params=_params(("arbitrary", "arbitrary")),
        name=name,
    )(be, nu, *args)


def _dense(x, w2d, col_offsets, *, tm, tn, n_out, out_dtype, epilogue, extras=(), name):
    nb = x.shape[0] // tm
    return _gmm(x, w2d[None], col_offsets, None, jnp.zeros((nb,), jnp.int32),
                jnp.full((1,), nb, jnp.int32), tm=tm, tn=tn, n_out=n_out, out_dtype=out_dtype,
                epilogue=epilogue, extras=extras, name=name)


def _emm_body(ge_ref, gb_ref, ns_ref, nu_ref, x_hbm, *refs, n_w, n_b, epilogue, sb, tn, n_blocks,
              cast_rows):
    w_refs = refs[:n_w]
    b_refs = refs[n_w:n_w + n_b]
    o_hbm = refs[n_w + n_b]
    xbuf, stage, pend, xsem, osem = refs[n_w + n_b + 1:n_w + n_b + 6]
    wb_refs = refs[n_w + n_b + 6:]
    g = pl.program_id(0)
    n = pl.program_id(1)
    nsub = ns_ref[g]
    blk0 = gb_ref[g]
    k = x_hbm.shape[1]
    n_col = o_hbm.shape[1] // tn

    n_groups = pl.num_programs(0)
    last_col = n == pl.num_programs(1) - 1
    g_next = jnp.minimum(g + 1, n_groups - 1)
    nsub_next = jnp.where(g + 1 < n_groups, ns_ref[g_next], 0)
    blk0_next = gb_ref[g_next]

    def x_copy(first_blk, s):
        row = pl.multiple_of((first_blk + s) * sb, sb)
        return pltpu.make_async_copy(x_hbm.at[pl.ds(row, sb), :], xbuf.at[s], xsem)

    def out_copy(half, row, col):
        return pltpu.make_async_copy(stage.at[half],
                                     o_hbm.at[pl.ds(pl.multiple_of(row, sb), sb),
                                              pl.ds(pl.multiple_of(col, tn), tn)], osem.at[half])

    def wait_half(half):
        @pl.when(pend[3 * half] == 1)
        def _():
            out_copy(half, pend[3 * half + 1], pend[3 * half + 2]).wait()
            pend[3 * half] = 0

    @pl.when(jnp.logical_and(g == 0, n == 0))
    def _():
        pend[0] = 0
        pend[3] = 0

        def start(s, carry):
            x_copy(blk0, s).start()
            return carry
        lax.fori_loop(0, nsub, start, 0)

    @pl.when(n == 0)
    def _():
        def wait(s, carry):
            x_copy(blk0, s).wait()
            return carry
        lax.fori_loop(0, nsub, wait, 0)

    @pl.when(nsub > 0)
    def _():
        def cast(r, carry):
            sl = pl.ds(pl.multiple_of(r * cast_rows, cast_rows), cast_rows)
            for w_ref, wb_ref in zip(w_refs, wb_refs):
                wb_ref[sl, :] = w_ref[sl, :].astype(BF16)
            return carry
        lax.fori_loop(0, k // cast_rows, cast, 0)

    def sub(s, carry):
        half = s % 2
        x = xbuf[s]
        accs = [jnp.dot(x, wb_ref[...], preferred_element_type=F32) for wb_ref in wb_refs]
        res = epilogue(accs, [b[...] for b in b_refs], []).astype(stage.dtype)
        wait_half(half)
        stage[half] = res
        row = (blk0 + s) * sb
        col = n * tn
        out_copy(half, row, col).start()
        pend[3 * half] = 1
        pend[3 * half + 1] = row
        pend[3 * half + 2] = col

        @pl.when(jnp.logical_and(last_col, s < nsub_next))
        def _():
            x_copy(blk0_next, s).start()
        return carry
    lax.fori_loop(0, nsub, sub, 0)

    @pl.when(last_col)
    def _():
        def start_rest(s, carry):
            x_copy(blk0_next, s).start()
            return carry
        lax.fori_loop(nsub, nsub_next, start_rest, 0)

    @pl.when(jnp.logical_and(g == pl.num_programs(0) - 1, n == pl.num_programs(1) - 1))
    def _():
        wait_half(0)
        wait_half(1)
        stage[0] = jnp.zeros(stage.shape[1:], stage.dtype)

        def zero_start(blk, carry):
            for c in range(n_col):
                out_copy(0, blk * sb, c * tn).start()
            return carry
        lax.fori_loop(nu_ref[0], n_blocks, zero_start, 0)

        def zero_wait(blk, carry):
            for c in range(n_col):
                out_copy(0, blk * sb, c * tn).wait()
            return carry
        lax.fori_loop(nu_ref[0], n_blocks, zero_wait, 0)


def _emm(x, w, col_offsets, bias, groups, nu, *, sb, n_sub_max, tn, n_out, out_dtype, epilogue, name):
    r, k = x.shape
    ge, gb, ns = groups
    n_groups = ge.shape[0]
    n_col = n_out // tn
    n_w = len(col_offsets)
    assert r % sb == 0 and n_out % tn == 0 and all(o % tn == 0 for o in col_offsets)

    def wmap(ob):
        def index(g, n, ge_r, gb_r, ns_r, nu_r):
            return ge_r[g], 0, jnp.where(ns_r[g] > 0, n, n_col - 1) + ob
        return index

    in_specs = [pl.BlockSpec(memory_space=pl.ANY)]
    args = [x]
    for off in col_offsets:
        in_specs.append(pl.BlockSpec((None, k, tn), wmap(off // tn)))
        args.append(w)
    for off in col_offsets:
        in_specs.append(pl.BlockSpec((None, 1, tn), wmap(off // tn)))
        args.append(bias)
    out_bytes = jnp.dtype(out_dtype).itemsize
    vmem = (n_sub_max * sb * k * 2 + n_w * (2 * k * tn * 4 + k * tn * 2) + n_w * sb * tn * 4
            + 2 * sb * tn * out_bytes)
    assert vmem <= VMEM_BUDGET_BYTES, (name, vmem)
    body = functools.partial(_emm_body, n_w=n_w, n_b=n_w, epilogue=epilogue, sb=sb, tn=tn,
                             n_blocks=r // sb, cast_rows=_tile(k, 256))
    return pl.pallas_call(
        body,
        grid_spec=pltpu.PrefetchScalarGridSpec(
            num_scalar_prefetch=4,
            grid=(n_groups, n_col),
            in_specs=in_specs,
            out_specs=pl.BlockSpec(memory_space=pl.ANY),
            scratch_shapes=[pltpu.VMEM((n_sub_max, sb, k), BF16),
                            pltpu.VMEM((2, sb, tn), out_dtype),
                            pltpu.SMEM((8,), jnp.int32),
                            pltpu.SemaphoreType.DMA(()),
                            pltpu.SemaphoreType.DMA((2,))]
                           + [pltpu.VMEM((k, tn), BF16) for _ in range(n_w)]),
        out_shape=jax.ShapeDtypeStruct((r, n_out), out_dtype),
        compiler_params=_params(("arbitrary", "arbitrary")),
        name=name,
    )(ge, gb, ns, nu, *args)


def _expert_groups(counts, pad_starts, *, sb, n_sub_max, n_blocks):
    n_experts = counts.shape[0]
    nblk = (counts + sb - 1) // sb
    ngrp = (nblk + n_sub_max - 1) // n_sub_max
    gend = jnp.cumsum(ngrp)
    gstart = gend - ngrp
    n_groups = n_experts + -(-n_blocks // n_sub_max)
    gid = jnp.arange(n_groups, dtype=jnp.int32)
    valid = gid < gend[-1]
    last = jnp.maximum(gend[-1] - 1, 0)
    ge = jnp.sum(gend[None, :] <= jnp.where(valid, gid, last)[:, None], axis=1)
    ge = jnp.minimum(ge, n_experts - 1).astype(jnp.int32)
    j = gid - gstart[ge]
    gb = pad_starts[ge] // sb + j * n_sub_max
    ns = jnp.where(valid, jnp.clip(nblk[ge] - j * n_sub_max, 0, n_sub_max), 0)
    return ge, gb.astype(jnp.int32), ns.astype(jnp.int32)


def _sigmoid(x):
    return 1.0 / (1.0 + jnp.exp(-x))


def _ep_plain(accs, biases, extras):
    return accs[0]


def _ep_glu(accs, biases, extras):
    return accs[0] * _sigmoid(accs[1])


def _ep_residual(accs, biases, extras):
    return accs[0] + extras[0]


def _ep_swiglu(accs, biases, extras):
    g = jnp.minimum(accs[0] + biases[0], SWIGLU_LIMIT)
    up = jnp.clip(accs[1] + biases[1], -SWIGLU_LIMIT, SWIGLU_LIMIT)
    return g * _sigmoid(SWIGLU_ALPHA * g) * (up + 1.0)


def _ep_bias(accs, biases, extras):
    return accs[0] + biases[0]


def _ep_ple(accs, biases, extras):
    h, p, wp = extras
    pp = jnp.dot(p, wp.astype(BF16), preferred_element_type=F32)
    return h + _sigmoid(accs[0]) * pp


def _loga_body(x_ref, wa_ref, wg_ref, bg_ref, o_ref):
    a = jnp.dot(x_ref[...], wa_ref[...], preferred_element_type=F32)
    z = jnp.dot(a.astype(BF16), wg_ref[...], preferred_element_type=F32) + bg_ref[...]
    log_sig = jnp.minimum(z, 0.0) - jnp.log1p(jnp.exp(-jnp.abs(z)))
    o_ref[...] = log_sig / GATE_NORMALIZER


def _log_decay(xn, w_alr, w_gate_lr, b_gate):
    t, d = xn.shape
    rank, qk = w_gate_lr.shape
    assert rank <= LANES
    wa = jnp.zeros((d, LANES), BF16).at[:, :rank].set(w_alr.astype(BF16))
    wg = jnp.zeros((LANES, qk), BF16).at[:rank, :].set(w_gate_lr.astype(BF16))
    tm = _tile(t, 512)
    return pl.pallas_call(
        _loga_body,
        grid=(t // tm,),
        in_specs=[pl.BlockSpec((tm, d), lambda i: (i, 0)),
                  pl.BlockSpec((d, LANES), lambda i: (0, 0)),
                  pl.BlockSpec((LANES, qk), lambda i: (0, 0)),
                  pl.BlockSpec((1, qk), lambda i: (0, 0))],
        out_specs=pl.BlockSpec((tm, qk), lambda i: (i, 0)),
        out_shape=jax.ShapeDtypeStruct((t, qk), F32),
        compiler_params=_params(("arbitrary",)),
        name="gla_log_decay",
    )(xn, wa, wg, b_gate.reshape(1, qk))


def _split3(x):
    hi = x.astype(BF16)
    r1 = x - hi.astype(F32)
    mid = r1.astype(BF16)
    lo = (r1 - mid.astype(F32)).astype(BF16)
    return hi, mid, lo


def _gla_body(q_ref, k_ref, v_ref, r_ref, la_ref, s0_ref, gn_ref, mix_in_ref, og_ref, sn_ref, s_ref,
              o_ref, *, chunk, n_sub, lc, scale):
    del mix_in_ref
    c = pl.program_id(1)

    @pl.when(c == 0)
    def _():
        s_ref[...] = s0_ref[...]

    nb, heads, dk, dv = s_ref.shape
    assert n_sub <= LANES
    rows = lax.broadcasted_iota(jnp.int32, (lc, lc), 0)
    cols = lax.broadcasted_iota(jnp.int32, (lc, lc), 1)
    same_chunk = (rows // chunk) == (cols // chunk)
    causal = jnp.logical_and(same_chunk, rows >= cols)
    tril = causal.astype(BF16)
    chunk_sum = same_chunk.astype(BF16)
    chunk_col = (lax.broadcasted_iota(jnp.int32, (lc, LANES), 0) // chunk
                 == lax.broadcasted_iota(jnp.int32, (lc, LANES), 1)).astype(BF16)
    tdims = (((0,), (0,)), ((), ()))

    chains = []
    for bb in range(nb):
        sl = slice(bb * lc, (bb + 1) * lc)
        for h in range(heads):
            kc = slice(h * dk, (h + 1) * dk)
            vc = slice(h * dv, (h + 1) * dv)
            q = q_ref[sl, kc] * scale
            k = k_ref[sl, kc]
            v = v_ref[sl, vc].astype(BF16)
            pieces = _split3(la_ref[sl, kc])
            b = sum(jnp.dot(tril, p, preferred_element_type=F32) for p in pieces)
            b_end = sum(jnp.dot(chunk_sum, p, preferred_element_type=F32) for p in pieces)
            b_end_col = sum(lax.dot_general(p, chunk_col, tdims, preferred_element_type=F32)
                            for p in pieces)
            decay = jnp.exp(b_end_col)
            qe = (q * jnp.exp(b)).astype(BF16)
            ke = (k * jnp.exp(-b)).astype(BF16)
            kd = (k * jnp.exp(b_end - b)).astype(BF16)
            att = lax.dot_general(qe, ke, (((1,), (1,)), ((), ())), preferred_element_type=F32)
            att = jnp.where(causal, att, 0.0).astype(BF16)
            o_intra = jnp.dot(att, v, preferred_element_type=F32)
            chains.append((bb, h, vc, qe, kd, v, decay, o_intra))

    for i in range(n_sub):
        cs = slice(i * chunk, (i + 1) * chunk)
        for bb, h, vc, qe, kd, v, decay, o_intra in chains:
            s = s_ref[bb, h]
            o = jnp.dot(qe[cs], s.astype(BF16), preferred_element_type=F32) + o_intra[cs]
            s_ref[bb, h] = (decay[:, i:i + 1] * s
                            + lax.dot_general(kd[cs], v[cs], tdims, preferred_element_type=F32))
            o_ref[bb * lc + i * chunk:bb * lc + (i + 1) * chunk, vc] = o

    o = o_ref[...]
    r = r_ref[...]
    for h in range(heads):
        vc = slice(h * dv, (h + 1) * dv)
        oh = o[:, vc]
        oh = oh * lax.rsqrt(jnp.mean(oh * oh, axis=-1, keepdims=True) + EPS) * gn_ref[:, vc]
        og_ref[:, vc] = (oh * (r[:, vc] * _sigmoid(r[:, vc]))).astype(og_ref.dtype)

    @pl.when(c == pl.num_programs(1) - 1)
    def _():
        sn_ref[...] = s_ref[...]


def _gla(qkvr, log_a, s0, gla_norm, mixed, *, row0, batch, length):
    _, heads, dk, dv = s0.shape
    qk, d_gla = heads * dk, heads * dv
    chunk = min(GLA_CHUNK, length)
    lc = _tile(length, 256, chunk)
    nc = length // lc
    nb = next(n for n in (2, 1) if batch % n == 0) if (nc == 1 and lc <= 16) else 1
    rows = nb * lc
    assert length % chunk == 0 and row0 % rows == 0 and (2 * qk) % d_gla == 0
    assert rows % 16 == 0
    rb0 = row0 // rows
    v_blk = 2 * qk // d_gla
    body = functools.partial(_gla_body, chunk=chunk, n_sub=lc // chunk, lc=lc, scale=dk ** -0.5)
    rowblk = lambda b, c: rb0 + b * nc + c
    og, s_new = pl.pallas_call(
        body,
        grid=(batch // nb, nc),
        in_specs=[pl.BlockSpec((rows, qk), lambda b, c: (rowblk(b, c), 0)),
                  pl.BlockSpec((rows, qk), lambda b, c: (rowblk(b, c), 1)),
                  pl.BlockSpec((rows, d_gla), lambda b, c: (rowblk(b, c), v_blk)),
                  pl.BlockSpec((rows, d_gla), lambda b, c: (rowblk(b, c), v_blk + 1)),
                  pl.BlockSpec((rows, qk), lambda b, c: (rowblk(b, c), 0)),
                  pl.BlockSpec((nb, heads, dk, dv), lambda b, c: (b, 0, 0, 0)),
                  pl.BlockSpec((1, d_gla), lambda b, c: (0, 0)),
                  pl.BlockSpec(memory_space=pl.ANY)],
        out_specs=[pl.BlockSpec((rows, d_gla), lambda b, c: (rowblk(b, c), 0)),
                   pl.BlockSpec((nb, heads, dk, dv), lambda b, c: (b, 0, 0, 0))],
        out_shape=[jax.ShapeDtypeStruct(mixed.shape, mixed.dtype),
                   jax.ShapeDtypeStruct((batch, heads, dk, dv), F32)],
        scratch_shapes=[pltpu.VMEM((nb, heads, dk, dv), F32), pltpu.VMEM((rows, d_gla), F32)],
        input_output_aliases={7: 0},
        compiler_params=_params(("arbitrary", "arbitrary")),
        name="gla",
    )(qkvr, qkvr, qkvr, qkvr, log_a, s0, gla_norm.reshape(1, d_gla), mixed)
    return og, s_new


def _conv_body(u_ref, st_ref, w_ref, b_ref, g_ref, beta_ref, mix_in_ref, c_ref, sn_ref, ext_ref, y_ref,
               sh_ref, *, width, lc, pad, row_tile, ch_tile):
    del mix_in_ref
    c = pl.program_id(1)
    hist = width - 1
    nb = st_ref.shape[0]
    ch = u_ref.shape[1]

    @pl.when(c == 0)
    def _():
        for bb in range(nb):
            if pad:
                ext_ref[bb, 0:pad, :] = jnp.zeros((pad, ch), F32)
            ext_ref[bb, pad:pad + hist, :] = st_ref[bb]

    base = pad + hist
    for bb in range(nb):
        ext_ref[bb, base:base + lc, :] = u_ref[bb * lc:(bb + 1) * lc, :]

    def taps(ci, carry):
        cs = pl.ds(pl.multiple_of(ci * ch_tile, ch_tile), ch_tile)
        for bb in range(nb):
            for r0 in range(0, lc, row_tile):
                acc = jnp.zeros((row_tile, ch_tile), F32) + b_ref[:, cs]
                for r in range(min(8, width)):
                    a_max = (width - 1 - r) // 8
                    start = pad + r + r0
                    span = 8 * a_max + row_tile
                    sh_ref[0:span, :] = ext_ref[bb, start:start + span, cs]
                    for a in range(a_max + 1):
                        j = 8 * a + r
                        acc = acc + w_ref[j:j + 1, cs] * sh_ref[8 * a:8 * a + row_tile, :]
                y_ref[bb * lc + r0:bb * lc + r0 + row_tile, cs] = acc
        return carry
    lax.fori_loop(0, ch // ch_tile, taps, 0)

    acc = y_ref[...]
    mu = jnp.mean(acc, axis=-1, keepdims=True)
    cen = acc - mu
    var = jnp.mean(cen * cen, axis=-1, keepdims=True)
    y = cen * lax.rsqrt(var + EPS) * g_ref[...] + beta_ref[...]
    c_ref[...] = (y * _sigmoid(y)).astype(c_ref.dtype)
    for bb in range(nb):
        tail = ext_ref[bb, pad + lc:pad + lc + hist, :]
        ext_ref[bb, pad:pad + hist, :] = tail

    @pl.when(c == pl.num_programs(1) - 1)
    def _():
        for bb in range(nb):
            sn_ref[bb] = ext_ref[bb, pad:pad + hist, :]


def _conv(u, state, conv_w, conv_b, ln_g, ln_b, mixed, *, row0, batch, length):
    ch = u.shape[1]
    width = conv_w.shape[0]
    hist = width - 1
    lc = _tile(length, 128)
    nc = length // lc
    nb = 2 if (nc == 1 and batch % 2 == 0) else 1
    rows = nb * lc
    assert row0 % rows == 0 and lc % 8 == 0 and mixed.shape[1] == 2 * ch
    rb0 = row0 // rows
    pad = (-hist) % 8
    row_tile = _tile(lc, 64)
    ch_tile = _tile(ch, 2 * LANES, LANES)
    body = functools.partial(_conv_body, width=width, lc=lc, pad=pad, row_tile=row_tile, ch_tile=ch_tile)
    vec = lambda a: a.reshape(1, ch)
    return pl.pallas_call(
        body,
        grid=(batch // nb, nc),
        in_specs=[pl.BlockSpec((rows, ch), lambda b, c: (rb0 + b * nc + c, 0)),
                  pl.BlockSpec((nb, hist, ch), lambda b, c: (b, 0, 0)),
                  pl.BlockSpec((width, ch), lambda b, c: (0, 0)),
                  pl.BlockSpec((1, ch), lambda b, c: (0, 0)),
                  pl.BlockSpec((1, ch), lambda b, c: (0, 0)),
                  pl.BlockSpec((1, ch), lambda b, c: (0, 0)),
                  pl.BlockSpec(memory_space=pl.ANY)],
        out_specs=[pl.BlockSpec((rows, ch), lambda b, c: (rb0 + b * nc + c, 1)),
                   pl.BlockSpec((nb, hist, ch), lambda b, c: (b, 0, 0))],
        out_shape=[jax.ShapeDtypeStruct(mixed.shape, mixed.dtype),
                   jax.ShapeDtypeStruct((batch, hist, ch), F32)],
        scratch_shapes=[pltpu.VMEM((nb, pad + hist + lc, ch), F32), pltpu.VMEM((rows, ch), F32),
                        pltpu.VMEM((8 * ((width - 1) // 8) + row_tile, ch_tile), F32)],
        input_output_aliases={6: 0},
        compiler_params=_params(("arbitrary", "arbitrary")),
        name="conformer_conv",
    )(u, state, conv_w, vec(conv_b), vec(ln_g), vec(ln_b), mixed)


HIGH_HALF = -(1 << 16)


def _pack_bf16_pairs(x):
    half = x.shape[1] // 2
    lo = lax.bitcast_convert_type(x[:, :half].astype(BF16).astype(F32), jnp.int32)
    hi = lax.bitcast_convert_type(x[:, half:].astype(BF16).astype(F32), jnp.int32)
    return lax.shift_right_logical(lo, 16) | (hi & HIGH_HALF)


def _unpack_bf16_pairs(w):
    lo = lax.bitcast_convert_type(lax.shift_left(w, 16), F32).astype(BF16)
    hi = lax.bitcast_convert_type(w & HIGH_HALF, F32).astype(BF16)
    return lo, hi


def _router_body(h_ref, g_ref, w_ref, b_ref, xn_ref, code_ref, gate_ref, cnt_ref, carry_ref,
                 *, n_experts):
    i = pl.program_id(0)

    @pl.when(i == 0)
    def _():
        carry_ref[...] = jnp.zeros(carry_ref.shape, F32)

    h = h_ref[...]
    xn = h * lax.rsqrt(jnp.mean(h * h, axis=-1, keepdims=True) + EPS) * g_ref[...]
    xn_ref[...] = _pack_bf16_pairs(xn)
    tm = h.shape[0]
    x_hi = xn.astype(BF16)
    x_lo = (xn - x_hi.astype(F32)).astype(BF16)
    w = w_ref[...]
    w_hi = w.astype(BF16)
    w_lo = (w - w_hi.astype(F32)).astype(BF16)
    logits = (jnp.dot(x_hi, w_hi, preferred_element_type=F32)
              + jnp.dot(x_lo, w_hi, preferred_element_type=F32)
              + jnp.dot(x_hi, w_lo, preferred_element_type=F32)) + b_ref[...]
    lane = lax.broadcasted_iota(jnp.int32, (tm, LANES), 1)
    lane_f = lane.astype(F32)
    work = jnp.where(lane < n_experts, logits, -jnp.inf)
    tops, hots = [], []
    for _ in range(TOP_K):
        m = jnp.max(work, axis=-1, keepdims=True)
        idx = jnp.min(jnp.where(work == m, lane_f, float(LANES)), axis=-1, keepdims=True)
        hot = lane_f == idx
        tops.append(m)
        hots.append(hot)
        work = jnp.where(hot, -jnp.inf, work)
    exps = [jnp.exp(t - tops[0]) for t in tops]
    denom = sum(exps)
    chosen = sum(hot.astype(F32) for hot in hots)
    rows = lax.broadcasted_iota(jnp.int32, (tm, tm), 0)
    cols = lax.broadcasted_iota(jnp.int32, (tm, tm), 1)
    before = (rows > cols).astype(BF16)
    seen = jnp.dot(before, chosen.astype(BF16), preferred_element_type=F32) + carry_ref[0:1, :]
    code = jnp.zeros((tm, LANES), jnp.int32)
    gate = jnp.zeros((tm, LANES), F32)
    for kk in range(TOP_K):
        hot_f = hots[kk].astype(F32)
        rank = jnp.sum(seen * hot_f, axis=-1, keepdims=True)
        expert = jnp.sum(lane_f * hot_f, axis=-1, keepdims=True)
        packed = expert.astype(jnp.int32) * (1 << RANK_BITS) + rank.astype(jnp.int32)
        code = jnp.where(lane == kk, packed, code)
        gate = jnp.where(lane == kk, exps[kk] / denom, gate)
    code_ref[...] = code
    gate_ref[...] = gate
    total = carry_ref[...] + jnp.sum(chosen, axis=0, keepdims=True)
    carry_ref[...] = total
    cnt_ref[...] = total


def _router(h, ln_g, w_router, b_router):
    t, d = h.shape
    n_experts = w_router.shape[1]
    assert n_experts <= LANES and t < (1 << RANK_BITS)
    wr = jnp.zeros((d, LANES), F32).at[:, :n_experts].set(w_router)
    br = jnp.zeros((1, LANES), F32).at[0, :n_experts].set(b_router)
    tm = _tile(t, 256)
    body = functools.partial(_router_body, n_experts=n_experts)
    return pl.pallas_call(
        body,
        grid=(t // tm,),
        in_specs=[pl.BlockSpec((tm, d), lambda i: (i, 0)),
                  pl.BlockSpec((1, d), lambda i: (0, 0)),
                  pl.BlockSpec((d, LANES), lambda i: (0, 0)),
                  pl.BlockSpec((1, LANES), lambda i: (0, 0))],
        out_specs=[pl.BlockSpec((tm, d // 2), lambda i: (i, 0)),
                   pl.BlockSpec((tm, LANES), lambda i: (i, 0)),
                   pl.BlockSpec((tm, LANES), lambda i: (i, 0)),
                   pl.BlockSpec((8, LANES), lambda i: (0, 0))],
        out_shape=[jax.ShapeDtypeStruct((t, d // 2), jnp.int32),
                   jax.ShapeDtypeStruct((t, LANES), jnp.int32),
                   jax.ShapeDtypeStruct((t, LANES), F32),
                   jax.ShapeDtypeStruct((8, LANES), F32)],
        scratch_shapes=[pltpu.VMEM((8, LANES), F32)],
        compiler_params=_params(("arbitrary",)),
        name="moe_router",
    )(h, ln_g.reshape(1, d), wr, br)


def _slot(code_ref, ps_ref, a):
    p = code_ref[a]
    return ps_ref[p >> RANK_BITS] + (p & ((1 << RANK_BITS) - 1))


def _dispatch_body(code_ref, ps_ref, cnt_ref, nu_ref, x_hbm, o_ref, tok_ref, buf_ref, sem,
                   *, n_tokens, n_experts, tm):
    i = pl.program_id(0)
    nu = nu_ref[0]

    def row_copy(blk, r):
        half = blk % 2
        return pltpu.make_async_copy(x_hbm.at[pl.ds(tok_ref[blk * tm + r], 1), :],
                                     buf_ref.at[half, pl.ds(r, 1), :], sem.at[half])

    def start_block(blk):
        def start(r8, carry):
            for lane in range(8):
                row_copy(blk, r8 * 8 + lane).start(priority=lane % 2)
            return carry
        lax.fori_loop(0, tm // 8, start, 0)

    @pl.when(i == 0)
    def _():
        def pad_expert(e, carry):
            first = ps_ref[e] + cnt_ref[e]
            last = ps_ref[e] + (cnt_ref[e] + tm - 1) // tm * tm

            def clear(s):
                tok_ref[s] = 0
                return s + 1
            lax.while_loop(lambda s: s < last, clear, first)
            return carry
        lax.fori_loop(0, n_experts, pad_expert, 0)

        def place(t):
            for kk in range(TOP_K):
                tok_ref[_slot(code_ref, ps_ref, t * TOP_K + kk)] = t
            return t + 1
        lax.while_loop(lambda t: t < n_tokens, place, 0)
        start_block(0)

    @pl.when(i + 1 < nu)
    def _():
        start_block(i + 1)

    @pl.when(i < nu)
    def _():
        def wait(r, carry):
            row_copy(i, r).wait()
            return carry
        lax.fori_loop(0, tm, wait, 0, unroll=8)
        half_d = buf_ref.shape[2]
        lo, hi = _unpack_bf16_pairs(buf_ref[i % 2])
        o_ref[:, :half_d] = lo
        o_ref[:, half_d:] = hi

    @pl.when(i >= nu)
    def _():
        o_ref[...] = jnp.zeros(o_ref.shape, o_ref.dtype)


def _dispatch(xn_packed, code, pad_starts, counts, nu, *, n_blocks, tm):
    t, half_d = xn_packed.shape
    d = 2 * half_d
    body = functools.partial(_dispatch_body, n_tokens=t, n_experts=counts.shape[0], tm=tm)
    return pl.pallas_call(
        body,
        grid_spec=pltpu.PrefetchScalarGridSpec(
            num_scalar_prefetch=4,
            grid=(n_blocks,),
            in_specs=[pl.BlockSpec(memory_space=pl.ANY)],
            out_specs=pl.BlockSpec((tm, d), lambda i, c_r, p_r, n_r, nu_r: (i, 0)),
            scratch_shapes=[pltpu.SMEM((n_blocks * tm,), jnp.int32),
                            pltpu.VMEM((2, tm, half_d), jnp.int32),
                            pltpu.SemaphoreType.DMA((2,))]),
        out_shape=jax.ShapeDtypeStruct((n_blocks * tm, d), BF16),
        compiler_params=_params(("arbitrary",)),
        name="moe_dispatch",
    )(code, pad_starts, counts, nu, xn_packed)


def _combine_body(code_ref, ps_ref, y_hbm, h_ref, gate_ref, g_ref, h_out_ref, xn_ref, buf_ref, sem, *, tm):
    i = pl.program_id(0)

    def row_copy(blk, r, kk):
        half = blk % 2
        slot = _slot(code_ref, ps_ref, (blk * tm + r) * TOP_K + kk)
        return pltpu.make_async_copy(y_hbm.at[pl.ds(slot, 1), :],
                                     buf_ref.at[half, kk, pl.ds(r, 1), :], sem.at[half])

    def start_block(blk):
        def start(r, carry):
            for kk in range(TOP_K):
                row_copy(blk, r, kk).start(priority=kk % 2)
            return carry
        lax.fori_loop(0, tm, start, 0, unroll=2)

    @pl.when(i == 0)
    def _():
        start_block(0)

    @pl.when(i + 1 < pl.num_programs(0))
    def _():
        start_block(i + 1)

    def wait(r, carry):
        for kk in range(TOP_K):
            row_copy(i, r, kk).wait()
        return carry
    lax.fori_loop(0, tm, wait, 0, unroll=2)

    gate = gate_ref[...]
    h = h_ref[...]
    for kk in range(TOP_K):
        h = h + gate[:, kk:kk + 1] * buf_ref[i % 2, kk]
    h_out_ref[...] = h
    xn = h * lax.rsqrt(jnp.mean(h * h, axis=-1, keepdims=True) + EPS) * g_ref[...]
    xn_ref[...] = xn.astype(xn_ref.dtype)


def _combine(yb, h, gate, code, pad_starts, ln_g):
    t, d = h.shape
    tm = _tile(t, 128)
    body = functools.partial(_combine_body, tm=tm)
    return pl.pallas_call(
        body,
        grid_spec=pltpu.PrefetchScalarGridSpec(
            num_scalar_prefetch=2,
            grid=(t // tm,),
            in_specs=[pl.BlockSpec(memory_space=pl.ANY),
                      pl.BlockSpec((tm, d), lambda i, c_r, p_r: (i, 0)),
                      pl.BlockSpec((tm, LANES), lambda i, c_r, p_r: (i, 0)),
                      pl.BlockSpec((1, d), lambda i, c_r, p_r: (0, 0))],
            out_specs=[pl.BlockSpec((tm, d), lambda i, c_r, p_r: (i, 0)),
                       pl.BlockSpec((tm, d), lambda i, c_r, p_r: (i, 0))],
            scratch_shapes=[pltpu.VMEM((2, TOP_K, tm, d), F32),
                            pltpu.SemaphoreType.DMA((2,))]),
        out_shape=[jax.ShapeDtypeStruct((t, d), F32),
                   jax.ShapeDtypeStruct((t, d), BF16)],
        compiler_params=_params(("arbitrary",)),
        name="moe_combine",
    )(code, pad_starts, yb, h, gate, ln_g.reshape(1, d))


def _layer(x, p, groups, s_gla, s_conv, ln_mix, w_in, w_gate_lr, b_gate, gla_norm, conv_w, conv_b,
           conv_ln_g, conv_ln_b, w_out, ln_moe, w_router, b_router, w_gu, b_gu, w_dn, b_dn,
           ln_ple, w_ple_gate, w_ple_proj):
    t, d = x.shape
    heads, dk, dv = s_gla[0].shape[1:]
    qk, d_gla = heads * dk, heads * dv
    rank = w_gate_lr.shape[0]
    d_conv = conv_w.shape[1]
    n_experts, _, d_ff2 = w_gu.shape
    d_ff = d_ff2 // 2
    n_qkvr = 2 * qk + 2 * d_gla
    glu0 = n_qkvr + rank
    tm_dense = _tile(t, 1024)
    tn = 512

    xn1 = _rmsnorm(x, ln_mix, BF16)
    qkvr = _dense(xn1, w_in[:, :n_qkvr], [0], tm=tm_dense, tn=_tile(n_qkvr, tn, LANES), n_out=n_qkvr, out_dtype=F32,
                  epilogue=_ep_plain, name="in_proj_qkvr")
    u = _dense(xn1, w_in[:, glu0:], [0, d_conv], tm=tm_dense, tn=_tile(d_conv, tn // 2, LANES), n_out=d_conv,
               out_dtype=F32, epilogue=_ep_glu, name="in_proj_glu")
    log_a = _log_decay(xn1, w_in[:, n_qkvr:glu0], w_gate_lr, b_gate)
    new_gla, new_conv = [], []
    assert d_gla == d_conv
    mixed = jnp.zeros((t, d_gla + d_conv), BF16)
    for (row0, batch, length), sg, sc in zip(groups, s_gla, s_conv):
        mixed, s_g = _gla(qkvr, log_a, sg, gla_norm, mixed, row0=row0, batch=batch, length=length)
        mixed, s_c = _conv(u, sc, conv_w, conv_b, conv_ln_g, conv_ln_b, mixed, row0=row0, batch=batch,
                           length=length)
        new_gla.append(s_g)
        new_conv.append(s_c)
    h1 = _dense(mixed, w_out, [0], tm=tm_dense, tn=_tile(d, tn, LANES), n_out=d, out_dtype=F32,
                epilogue=_ep_residual, extras=[(x, "tile")], name="out_proj")

    xn2, code, gate, counts = _router(h1, ln_moe, w_router, b_router)
    tm_e = _tile(t, EXPERT_ROWS)
    counts = counts[0, :n_experts].astype(jnp.int32)
    padded = (counts + tm_e - 1) // tm_e * tm_e
    pad_ends = jnp.cumsum(padded)
    pad_starts = (pad_ends - padded).astype(jnp.int32)
    n_blocks = -(-t * TOP_K // tm_e) + n_experts
    nu = (pad_ends[-1:] // tm_e).astype(jnp.int32)
    groups = _expert_groups(counts, pad_starts, sb=tm_e, n_sub_max=GROUP_BLOCKS, n_blocks=n_blocks)
    code = code[:, :TOP_K].reshape(-1)
    xb = _dispatch(xn2, code, pad_starts, counts, nu, n_blocks=n_blocks, tm=tm_e)
    act = _emm(xb, w_gu, [0, d_ff], b_gu.reshape(n_experts, 1, d_ff2), groups, nu, sb=tm_e,
               n_sub_max=GROUP_BLOCKS, tn=_tile(d_ff, tn, LANES), n_out=d_ff, out_dtype=BF16,
               epilogue=_ep_swiglu, name="moe_gate_up")
    yb = _emm(act, w_dn, [0], b_dn.reshape(n_experts, 1, d), groups, nu, sb=tm_e,
              n_sub_max=GROUP_BLOCKS, tn=_tile(d, 2 * tn, LANES), n_out=d, out_dtype=F32,
              epilogue=_ep_bias, name="moe_down")
    h2, xn3 = _combine(yb, h1, gate, code, pad_starts, ln_ple)

    h3 = _dense(xn3, w_ple_gate, [0], tm=_tile(t, 512), tn=_tile(d, tn, LANES), n_out=d, out_dtype=F32,
                epilogue=_ep_ple, extras=[(h2, "tile"), (p.astype(BF16), "row"), (w_ple_proj, "col")],
                name="ple")
    return h3, new_gla, new_conv


def kernel(x_prompt, x_sample, p_prompt, p_sample, state_gla, state_conv, ln_mix, w_in, w_gate_lr, b_gate, gla_norm, conv_w, conv_b, conv_ln_g, conv_ln_b, w_out, ln_moe, w_router, b_router, w_gu, b_gu, w_dn, b_dn, ln_ple, w_ple_gate, w_ple_proj, ln_final):
    bp, lp, d = x_prompt.shape
    bs, ls, _ = x_sample.shape
    depth = state_gla.shape[0]
    tp, ts = bp * lp, bs * ls
    groups = [(0, bp, lp), (tp, bs, ls)]
    h = jnp.concatenate([x_prompt.reshape(tp, d), x_sample.reshape(ts, d)], axis=0)
    zero_gla = jnp.zeros((bp,) + state_gla.shape[2:], state_gla.dtype)
    zero_conv = jnp.zeros((bp,) + state_conv.shape[2:], state_conv.dtype)
    gla_p, gla_s, conv_p, conv_s = [], [], [], []
    for i in range(depth):
        p = jnp.concatenate([p_prompt[i].reshape(tp, -1), p_sample[i].reshape(ts, -1)], axis=0)
        h, new_gla, new_conv = _layer(
            h, p, groups, [zero_gla, state_gla[i]], [zero_conv, state_conv[i]], ln_mix[i], w_in[i],
            w_gate_lr[i], b_gate[i], gla_norm[i], conv_w[i], conv_b[i], conv_ln_g[i], conv_ln_b[i],
            w_out[i], ln_moe[i], w_router[i], b_router[i], w_gu[i], b_gu[i], w_dn[i], b_dn[i],
            ln_ple[i], w_ple_gate[i], w_ple_proj[i])
        gla_p.append(new_gla[0])
        gla_s.append(new_gla[1])
        conv_p.append(new_conv[0])
        conv_s.append(new_conv[1])
    y_prompt = _rmsnorm(h, ln_final, F32, row0=0, rows=tp).reshape(bp, lp, d)
    y_sample = _rmsnorm(h, ln_final, F32, row0=tp, rows=ts).reshape(bs, ls, d)
    return (y_prompt, y_sample, jnp.stack(gla_p, 0), jnp.stack(conv_p, 0),
            jnp.stack(gla_s, 0), jnp.stack(conv_s, 0))
```

```python
import functools

import jax
import jax.numpy as jnp
from jax import lax
from jax.experimental import pallas as pl
from jax.experimental.pallas import tpu as pltpu

EPS = 1e-6
TOP_K = 4
GATE_NORMALIZER = 16.0
GLA_CHUNK = 16
SWIGLU_LIMIT = 7.0
SWIGLU_ALPHA = 1.702

LANES = 128
VMEM_LIMIT_BYTES = 60 * 2**20
VMEM_BUDGET_BYTES = 54 * 2**20
RANK_BITS = 16
EXPERT_ROWS = 256
GROUP_BLOCKS = 5

F32 = jnp.float32
BF16 = jnp.bfloat16


def _tile(n, pref, align=8):
    if n <= pref:
        return n
    for t in range(pref, 0, -1):
        if n % t == 0 and t % align == 0:
            return t
    return n


def _params(sem):
    return pltpu.CompilerParams(dimension_semantics=sem, vmem_limit_bytes=VMEM_LIMIT_BYTES)


def _rms_body(x_ref, g_ref, o_ref):
    x = x_ref[...]
    y = x * lax.rsqrt(jnp.mean(x * x, axis=-1, keepdims=True) + EPS)
    o_ref[...] = (y * g_ref[...]).astype(o_ref.dtype)


def _rmsnorm(x, g, out_dtype, row0=0, rows=None):
    rows = x.shape[0] if rows is None else rows
    d = x.shape[1]
    tm = _tile(rows, 512)
    assert row0 % tm == 0
    off = row0 // tm
    return pl.pallas_call(
        _rms_body,
        grid=(rows // tm,),
        in_specs=[pl.BlockSpec((tm, d), lambda i: (i + off, 0)),
                  pl.BlockSpec((1, d), lambda i: (0, 0))],
        out_specs=pl.BlockSpec((tm, d), lambda i: (i, 0)),
        out_shape=jax.ShapeDtypeStruct((rows, d), out_dtype),
        compiler_params=_params(("arbitrary",)),
        name="rmsnorm",
    )(x, g.reshape(1, d))


def _gmm_body(be_ref, nu_ref, *refs, n_w, n_b, n_extra, epilogue, cast_rows):
    x_ref = refs[0]
    w_refs = refs[1:1 + n_w]
    b_refs = refs[1 + n_w:1 + n_w + n_b]
    e_refs = refs[1 + n_w + n_b:1 + n_w + n_b + n_extra]
    o_ref = refs[1 + n_w + n_b + n_extra]
    wb_refs = refs[2 + n_w + n_b + n_extra:]
    i = pl.program_id(1)
    valid = i < nu_ref[0]
    prev = jnp.maximum(i - 1, 0)
    changed = jnp.logical_or(i == 0, be_ref[i] != be_ref[prev])
    k = w_refs[0].shape[0]

    @pl.when(jnp.logical_and(valid, changed))
    def _():
        def cast(r, carry):
            sl = pl.ds(pl.multiple_of(r * cast_rows, cast_rows), cast_rows)
            for w_ref, wb_ref in zip(w_refs, wb_refs):
                wb_ref[sl, :] = w_ref[sl, :].astype(BF16)
            return carry
        lax.fori_loop(0, k // cast_rows, cast, 0)

    @pl.when(valid)
    def _():
        x = x_ref[...]
        accs = [jnp.dot(x, wb_ref[...], preferred_element_type=F32) for wb_ref in wb_refs]
        o_ref[...] = epilogue(accs, [b[...] for b in b_refs], [e[...] for e in e_refs]).astype(o_ref.dtype)

    @pl.when(jnp.logical_not(valid))
    def _():
        o_ref[...] = jnp.zeros(o_ref.shape, o_ref.dtype)


def _gmm(x, w, col_offsets, bias, be, nu, *, tm, tn, n_out, out_dtype, epilogue, extras=(), name):
    r, k = x.shape
    nb = r // tm
    assert r % tm == 0 and n_out % tn == 0 and all(o % tn == 0 for o in col_offsets)
    n_w = len(col_offsets)

    def row(i, nu_ref):
        return jnp.minimum(i, nu_ref[0] - 1)

    in_specs = [pl.BlockSpec((tm, k), lambda j, i, be_r, nu_r: (row(i, nu_r), 0))]
    args = [x]
    for off in col_offsets:
        ob = off // tn
        in_specs.append(pl.BlockSpec((None, k, tn),
                                     lambda j, i, be_r, nu_r, ob=ob: (be_r[row(i, nu_r)], 0, j + ob)))
        args.append(w)
    n_b = 0
    if bias is not None:
        for off in col_offsets:
            ob = off // tn
            in_specs.append(pl.BlockSpec((None, 1, tn),
                                         lambda j, i, be_r, nu_r, ob=ob: (be_r[row(i, nu_r)], 0, j + ob)))
            args.append(bias)
            n_b += 1
    for arr, kind in extras:
        if kind == "tile":
            in_specs.append(pl.BlockSpec((tm, tn), lambda j, i, be_r, nu_r: (row(i, nu_r), j)))
        elif kind == "row":
            in_specs.append(pl.BlockSpec((tm, arr.shape[1]), lambda j, i, be_r, nu_r: (row(i, nu_r), 0)))
        else:
            in_specs.append(pl.BlockSpec((arr.shape[0], tn), lambda j, i, be_r, nu_r: (0, j)))
        args.append(arr)
    vmem = (2 * tm * k * 2 + n_w * (2 * k * tn * 4 + k * tn * 2) + n_w * tm * tn * 4
            + 2 * tm * tn * jnp.dtype(out_dtype).itemsize)
    for arr, kind in extras:
        blk = {"tile": tm * tn, "row": tm * arr.shape[1], "col": arr.shape[0] * tn}[kind]
        vmem += 2 * blk * arr.dtype.itemsize
    assert vmem <= VMEM_BUDGET_BYTES, (name, vmem)
    cast_rows = _tile(k, 256)
    body = functools.partial(_gmm_body, n_w=n_w, n_b=n_b, n_extra=len(extras), epilogue=epilogue,
                             cast_rows=cast_rows)
    return pl.pallas_call(
        body,
        grid_spec=pltpu.PrefetchScalarGridSpec(
            num_scalar_prefetch=2,
            grid=(n_out // tn, nb),
            in_specs=in_specs,
            out_specs=pl.BlockSpec((tm, tn), lambda j, i, be_r, nu_r: (i, j)),
            scratch_shapes=[pltpu.VMEM((k, tn), BF16) for _ in range(n_w)]),
        out_shape=jax.ShapeDtypeStruct((r, n_out), out_dtype),
        compiler_params=_params(("arbitrary", "arbitrary")),
        name=name,
    )(be, nu, *args)


def _dense(x, w2d, col_offsets, *, tm, tn, n_out, out_dtype, epilogue, extras=(), name):
    nb = x.shape[0] // tm
    return _gmm(x, w2d[None], col_offsets, None, jnp.zeros((nb,), jnp.int32),
                jnp.full((1,), nb, jnp.int32), tm=tm, tn=tn, n_out=n_out, out_dtype=out_dtype,
                epilogue=epilogue, extras=extras, name=name)


def _emm_body(ge_ref, gb_ref, ns_ref, nu_ref, x_hbm, *refs, n_w, n_b, epilogue, sb, tn, n_blocks,
              cast_rows):
    w_refs = refs[:n_w]
    b_refs = refs[n_w:n_w + n_b]
    o_hbm = refs[n_w + n_b]
    xbuf, stage, pend, xsem, osem = refs[n_w + n_b + 1:n_w + n_b + 6]
    wb_refs = refs[n_w + n_b + 6:]
    g = pl.program_id(0)
    n = pl.program_id(1)
    nsub = ns_ref[g]
    blk0 = gb_ref[g]
    k = x_hbm.shape[1]
    n_col = o_hbm.shape[1] // tn

    n_groups = pl.num_programs(0)
    last_col = n == pl.num_programs(1) - 1
    g_next = jnp.minimum(g + 1, n_groups - 1)
    nsub_next = jnp.where(g + 1 < n_groups, ns_ref[g_next], 0)
    blk0_next = gb_ref[g_next]

    def x_copy(first_blk, s):
        row = pl.multiple_of((first_blk + s) * sb, sb)
        return pltpu.make_async_copy(x_hbm.at[pl.ds(row, sb), :], xbuf.at[s], xsem)

    def out_copy(half, row, col):
        return pltpu.make_async_copy(stage.at[half],
                                     o_hbm.at[pl.ds(pl.multiple_of(row, sb), sb),
                                              pl.ds(pl.multiple_of(col, tn), tn)], osem.at[half])

    def wait_half(half):
        @pl.when(pend[3 * half] == 1)
        def _():
            out_copy(half, pend[3 * half + 1], pend[3 * half + 2]).wait()
            pend[3 * half] = 0

    @pl.when(jnp.logical_and(g == 0, n == 0))
    def _():
        pend[0] = 0
        pend[3] = 0

        def start(s, carry):
            x_copy(blk0, s).start()
            return carry
        lax.fori_loop(0, nsub, start, 0)

    @pl.when(n == 0)
    def _():
        def wait(s, carry):
            x_copy(blk0, s).wait()
            return carry
        lax.fori_loop(0, nsub, wait, 0)

    @pl.when(nsub > 0)
    def _():
        def cast(r, carry):
            sl = pl.ds(pl.multiple_of(r * cast_rows, cast_rows), cast_rows)
            for w_ref, wb_ref in zip(w_refs, wb_refs):
                wb_ref[sl, :] = w_ref[sl, :].astype(BF16)
            return carry
        lax.fori_loop(0, k // cast_rows, cast, 0)

    def sub(s, carry):
        half = s % 2
        x = xbuf[s]
        accs = [jnp.dot(x, wb_ref[...], preferred_element_type=F32) for wb_ref in wb_refs]
        res = epilogue(accs, [b[...] for b in b_refs], []).astype(stage.dtype)
        wait_half(half)
        stage[half] = res
        row = (blk0 + s) * sb
        col = n * tn
        out_copy(half, row, col).start()
        pend[3 * half] = 1
        pend[3 * half + 1] = row
        pend[3 * half + 2] = col

        @pl.when(jnp.logical_and(last_col, s < nsub_next))
        def _():
            x_copy(blk0_next, s).start()
        return carry
    lax.fori_loop(0, nsub, sub, 0)

    @pl.when(last_col)
    def _():
        def start_rest(s, carry):
            x_copy(blk0_next, s).start()
            return carry
        lax.fori_loop(nsub, nsub_next, start_rest, 0)

    @pl.when(jnp.logical_and(g == pl.num_programs(0) - 1, n == pl.num_programs(1) - 1))
    def _():
        wait_half(0)
        wait_half(1)
        stage[0] = jnp.zeros(stage.shape[1:], stage.dtype)

        def zero_start(blk, carry):
            for c in range(n_col):
                out_copy(0, blk * sb, c * tn).start()
            return carry
        lax.fori_loop(nu_ref[0], n_blocks, zero_start, 0)

        def zero_wait(blk, carry):
            for c in range(n_col):
                out_copy(0, blk * sb, c * tn).wait()
            return carry
        lax.fori_loop(nu_ref[0], n_blocks, zero_wait, 0)


def _emm(x, w, col_offsets, bias, groups, nu, *, sb, n_sub_max, tn, n_out, out_dtype, epilogue, name):
    r, k = x.shape
    ge, gb, ns = groups
    n_groups = ge.shape[0]
    n_col = n_out // tn
    n_w = len(col_offsets)
    assert r % sb == 0 and n_out % tn == 0 and all(o % tn == 0 for o in col_offsets)

    def wmap(ob):
        def index(g, n, ge_r, gb_r, ns_r, nu_r):
            return ge_r[g], 0, jnp.where(ns_r[g] > 0, n, n_col - 1) + ob
        return index

    in_specs = [pl.BlockSpec(memory_space=pl.ANY)]
    args = [x]
    for off in col_offsets:
        in_specs.append(pl.BlockSpec((None, k, tn), wmap(off // tn)))
        args.append(w)
    for off in col_offsets:
        in_specs.append(pl.BlockSpec((None, 1, tn), wmap(off // tn)))
        args.append(bias)
    out_bytes = jnp.dtype(out_dtype).itemsize
    vmem = (n_sub_max * sb * k * 2 + n_w * (2 * k * tn * 4 + k * tn * 2) + n_w * sb * tn * 4
            + 2 * sb * tn * out_bytes)
    assert vmem <= VMEM_BUDGET_BYTES, (name, vmem)
    body = functools.partial(_emm_body, n_w=n_w, n_b=n_w, epilogue=epilogue, sb=sb, tn=tn,
                             n_blocks=r // sb, cast_rows=_tile(k, 256))
    return pl.pallas_call(
        body,
        grid_spec=pltpu.PrefetchScalarGridSpec(
            num_scalar_prefetch=4,
            grid=(n_groups, n_col),
            in_specs=in_specs,
            out_specs=pl.BlockSpec(memory_space=pl.ANY),
            scratch_shapes=[pltpu.VMEM((n_sub_max, sb, k), BF16),
                            pltpu.VMEM((2, sb, tn), out_dtype),
                            pltpu.SMEM((8,), jnp.int32),
                            pltpu.SemaphoreType.DMA(()),
                            pltpu.SemaphoreType.DMA((2,))]
                           + [pltpu.VMEM((k, tn), BF16) for _ in range(n_w)]),
        out_shape=jax.ShapeDtypeStruct((r, n_out), out_dtype),
        compiler_params=_params(("arbitrary", "arbitrary")),
        name=name,
    )(ge, gb, ns, nu, *args)


def _expert_groups(counts, pad_starts, *, sb, n_sub_max, n_blocks):
    n_experts = counts.shape[0]
    nblk = (counts + sb - 1) // sb
    ngrp = (nblk + n_sub_max - 1) // n_sub_max
    gend = jnp.cumsum(ngrp)
    gstart = gend - ngrp
    n_groups = n_experts + -(-n_blocks // n_sub_max)
    gid = jnp.arange(n_groups, dtype=jnp.int32)
    valid = gid < gend[-1]
    last = jnp.maximum(gend[-1] - 1, 0)
    ge = jnp.sum(gend[None, :] <= jnp.where(valid, gid, last)[:, None], axis=1)
    ge = jnp.minimum(ge, n_experts - 1).astype(jnp.int32)
    j = gid - gstart[ge]
    gb = pad_starts[ge] // sb + j * n_sub_max
    ns = jnp.where(valid, jnp.clip(nblk[ge] - j * n_sub_max, 0, n_sub_max), 0)
    return ge, gb.astype(jnp.int32), ns.astype(jnp.int32)


def _sigmoid(x):
    return 1.0 / (1.0 + jnp.exp(-x))


def _ep_plain(accs, biases, extras):
    return accs[0]


def _ep_glu(accs, biases, extras):
    return accs[0] * _sigmoid(accs[1])


def _ep_residual(accs, biases, extras):
    return accs[0] + extras[0]


def _ep_swiglu(accs, biases, extras):
    g = jnp.minimum(accs[0] + biases[0], SWIGLU_LIMIT)
    up = jnp.clip(accs[1] + biases[1], -SWIGLU_LIMIT, SWIGLU_LIMIT)
    return g * _sigmoid(SWIGLU_ALPHA * g) * (up + 1.0)


def _ep_bias(accs, biases, extras):
    return accs[0] + biases[0]


def _ep_ple(accs, biases, extras):
    h, p, wp = extras
    pp = jnp.dot(p, wp.astype(BF16), preferred_element_type=F32)
    return h + _sigmoid(accs[0]) * pp


def _loga_body(x_ref, wa_ref, wg_ref, bg_ref, o_ref):
    a = jnp.dot(x_ref[...], wa_ref[...], preferred_element_type=F32)
    z = jnp.dot(a.astype(BF16), wg_ref[...], preferred_element_type=F32) + bg_ref[...]
    log_sig = jnp.minimum(z, 0.0) - jnp.log1p(jnp.exp(-jnp.abs(z)))
    o_ref[...] = log_sig / GATE_NORMALIZER


def _log_decay(xn, w_alr, w_gate_lr, b_gate):
    t, d = xn.shape
    rank, qk = w_gate_lr.shape
    assert rank <= LANES
    wa = jnp.zeros((d, LANES), BF16).at[:, :rank].set(w_alr.astype(BF16))
    wg = jnp.zeros((LANES, qk), BF16).at[:rank, :].set(w_gate_lr.astype(BF16))
    tm = _tile(t, 512)
    return pl.pallas_call(
        _loga_body,
        grid=(t // tm,),
        in_specs=[pl.BlockSpec((tm, d), lambda i: (i, 0)),
                  pl.BlockSpec((d, LANES), lambda i: (0, 0)),
                  pl.BlockSpec((LANES, qk), lambda i: (0, 0)),
                  pl.BlockSpec((1, qk), lambda i: (0, 0))],
        out_specs=pl.BlockSpec((tm, qk), lambda i: (i, 0)),
        out_shape=jax.ShapeDtypeStruct((t, qk), F32),
        compiler_params=_params(("arbitrary",)),
        name="gla_log_decay",
    )(xn, wa, wg, b_gate.reshape(1, qk))


def _split3(x):
    hi = x.astype(BF16)
    r1 = x - hi.astype(F32)
    mid = r1.astype(BF16)
    lo = (r1 - mid.astype(F32)).astype(BF16)
    return hi, mid, lo


def _gla_body(q_ref, k_ref, v_ref, r_ref, la_ref, s0_ref, gn_ref, mix_in_ref, og_ref, sn_ref, s_ref,
              o_ref, *, chunk, n_sub, lc, scale):
    del mix_in_ref
    c = pl.program_id(1)

    @pl.when(c == 0)
    def _():
        s_ref[...] = s0_ref[...]

    nb, heads, dk, dv = s_ref.shape
    assert n_sub <= LANES
    rows = lax.broadcasted_iota(jnp.int32, (lc, lc), 0)
    cols = lax.broadcasted_iota(jnp.int32, (lc, lc), 1)
    same_chunk = (rows // chunk) == (cols // chunk)
    causal = jnp.logical_and(same_chunk, rows >= cols)
    tril = causal.astype(BF16)
    chunk_sum = same_chunk.astype(BF16)
    chunk_col = (lax.broadcasted_iota(jnp.int32, (lc, LANES), 0) // chunk
                 == lax.broadcasted_iota(jnp.int32, (lc, LANES), 1)).astype(BF16)
    tdims = (((0,), (0,)), ((), ()))

    chains = []
    for bb in range(nb):
        sl = slice(bb * lc, (bb + 1) * lc)
        for h in range(heads):
            kc = slice(h * dk, (h + 1) * dk)
            vc = slice(h * dv, (h + 1) * dv)
            q = q_ref[sl, kc] * scale
            k = k_ref[sl, kc]
            v = v_ref[sl, vc].astype(BF16)
            pieces = _split3(la_ref[sl, kc])
            b = sum(jnp.dot(tril, p, preferred_element_type=F32) for p in pieces)
            b_end = sum(jnp.dot(chunk_sum, p, preferred_element_type=F32) for p in pieces)
            b_end_col = sum(lax.dot_general(p, chunk_col, tdims, preferred_element_type=F32)
                            for p in pieces)
            decay = jnp.exp(b_end_col)
            qe = (q * jnp.exp(b)).astype(BF16)
            ke = (k * jnp.exp(-b)).astype(BF16)
            kd = (k * jnp.exp(b_end - b)).astype(BF16)
            att = lax.dot_general(qe, ke, (((1,), (1,)), ((), ())), preferred_element_type=F32)
            att = jnp.where(causal, att, 0.0).astype(BF16)
            o_intra = jnp.dot(att, v, preferred_element_type=F32)
            chains.append((bb, h, vc, qe, kd, v, decay, o_intra))

    for i in range(n_sub):
        cs = slice(i * chunk, (i + 1) * chunk)
        for bb, h, vc, qe, kd, v, decay, o_intra in chains:
            s = s_ref[bb, h]
            o = jnp.dot(qe[cs], s.astype(BF16), preferred_element_type=F32) + o_intra[cs]
            s_ref[bb, h] = (decay[:, i:i + 1] * s
                            + lax.dot_general(kd[cs], v[cs], tdims, preferred_element_type=F32))
            o_ref[bb * lc + i * chunk:bb * lc + (i + 1) * chunk, vc] = o

    o = o_ref[...]
    r = r_ref[...]
    for h in range(heads):
        vc = slice(h * dv, (h + 1) * dv)
        oh = o[:, vc]
        oh = oh * lax.rsqrt(jnp.mean(oh * oh, axis=-1, keepdims=True) + EPS) * gn_ref[:, vc]
        og_ref[:, vc] = (oh * (r[:, vc] * _sigmoid(r[:, vc]))).astype(og_ref.dtype)

    @pl.when(c == pl.num_programs(1) - 1)
    def _():
        sn_ref[...] = s_ref[...]


def _gla(qkvr, log_a, s0, gla_norm, mixed, *, row0, batch, length):
    _, heads, dk, dv = s0.shape
    qk, d_gla = heads * dk, heads * dv
    chunk = min(GLA_CHUNK, length)
    lc = _tile(length, 256, chunk)
    nc = length // lc
    nb = next(n for n in (2, 1) if batch % n == 0) if (nc == 1 and lc <= 16) else 1
    rows = nb * lc
    assert length % chunk == 0 and row0 % rows == 0 and (2 * qk) % d_gla == 0
    assert rows % 16 == 0
    rb0 = row0 // rows
    v_blk = 2 * qk // d_gla
    body = functools.partial(_gla_body, chunk=chunk, n_sub=lc // chunk, lc=lc, scale=dk ** -0.5)
    rowblk = lambda b, c: rb0 + b * nc + c
    og, s_new = pl.pallas_call(
        body,
        grid=(batch // nb, nc),
        in_specs=[pl.BlockSpec((rows, qk), lambda b, c: (rowblk(b, c), 0)),
                  pl.BlockSpec((rows, qk), lambda b, c: (rowblk(b, c), 1)),
                  pl.BlockSpec((rows, d_gla), lambda b, c: (rowblk(b, c), v_blk)),
                  pl.BlockSpec((rows, d_gla), lambda b, c: (rowblk(b, c), v_blk + 1)),
                  pl.BlockSpec((rows, qk), lambda b, c: (rowblk(b, c), 0)),
                  pl.BlockSpec((nb, heads, dk, dv), lambda b, c: (b, 0, 0, 0)),
                  pl.BlockSpec((1, d_gla), lambda b, c: (0, 0)),
                  pl.BlockSpec(memory_space=pl.ANY)],
        out_specs=[pl.BlockSpec((rows, d_gla), lambda b, c: (rowblk(b, c), 0)),
                   pl.BlockSpec((nb, heads, dk, dv), lambda b, c: (b, 0, 0, 0))],
        out_shape=[jax.ShapeDtypeStruct(mixed.shape, mixed.dtype),
                   jax.ShapeDtypeStruct((batch, heads, dk, dv), F32)],
        scratch_shapes=[pltpu.VMEM((nb, heads, dk, dv), F32), pltpu.VMEM((rows, d_gla), F32)],
        input_output_aliases={7: 0},
        compiler_params=_params(("arbitrary", "arbitrary")),
        name="gla",
    )(qkvr, qkvr, qkvr, qkvr, log_a, s0, gla_norm.reshape(1, d_gla), mixed)
    return og, s_new


def _conv_body(u_ref, st_ref, w_ref, b_ref, g_ref, beta_ref, mix_in_ref, c_ref, sn_ref, ext_ref, y_ref,
               sh_ref, *, width, lc, pad, row_tile, ch_tile):
    del mix_in_ref
    c = pl.program_id(1)
    hist = width - 1
    nb = st_ref.shape[0]
    ch = u_ref.shape[1]

    @pl.when(c == 0)
    def _():
        for bb in range(nb):
            if pad:
                ext_ref[bb, 0:pad, :] = jnp.zeros((pad, ch), F32)
            ext_ref[bb, pad:pad + hist, :] = st_ref[bb]

    base = pad + hist
    for bb in range(nb):
        ext_ref[bb, base:base + lc, :] = u_ref[bb * lc:(bb + 1) * lc, :]

    def taps(ci, carry):
        cs = pl.ds(pl.multiple_of(ci * ch_tile, ch_tile), ch_tile)
        for bb in range(nb):
            for r0 in range(0, lc, row_tile):
                acc = jnp.zeros((row_tile, ch_tile), F32) + b_ref[:, cs]
                for r in range(min(8, width)):
                    a_max = (width - 1 - r) // 8
                    start = pad + r + r0
                    span = 8 * a_max + row_tile
                    sh_ref[0:span, :] = ext_ref[bb, start:start + span, cs]
                    for a in range(a_max + 1):
                        j = 8 * a + r
                        acc = acc + w_ref[j:j + 1, cs] * sh_ref[8 * a:8 * a + row_tile, :]
                y_ref[bb * lc + r0:bb * lc + r0 + row_tile, cs] = acc
        return carry
    lax.fori_loop(0, ch // ch_tile, taps, 0)

    acc = y_ref[...]
    mu = jnp.mean(acc, axis=-1, keepdims=True)
    cen = acc - mu
    var = jnp.mean(cen * cen, axis=-1, keepdims=True)
    y = cen * lax.rsqrt(var + EPS) * g_ref[...] + beta_ref[...]
    c_ref[...] = (y * _sigmoid(y)).astype(c_ref.dtype)
    for bb in range(nb):
        tail = ext_ref[bb, pad + lc:pad + lc + hist, :]
        ext_ref[bb, pad:pad + hist, :] = tail

    @pl.when(c == pl.num_programs(1) - 1)
    def _():
        for bb in range(nb):
            sn_ref[bb] = ext_ref[bb, pad:pad + hist, :]


def _conv(u, state, conv_w, conv_b, ln_g, ln_b, mixed, *, row0, batch, length):
    ch = u.shape[1]
    width = conv_w.shape[0]
    hist = width - 1
    lc = _tile(length, 128)
    nc = length // lc
    nb = 2 if (nc == 1 and batch % 2 == 0) else 1
    rows = nb * lc
    assert row0 % rows == 0 and lc % 8 == 0 and mixed.shape[1] == 2 * ch
    rb0 = row0 // rows
    pad = (-hist) % 8
    row_tile = _tile(lc, 64)
    ch_tile = _tile(ch, 2 * LANES, LANES)
    body = functools.partial(_conv_body, width=width, lc=lc, pad=pad, row_tile=row_tile, ch_tile=ch_tile)
    vec = lambda a: a.reshape(1, ch)
    return pl.pallas_call(
        body,
        grid=(batch // nb, nc),
        in_specs=[pl.BlockSpec((rows, ch), lambda b, c: (rb0 + b * nc + c, 0)),
                  pl.BlockSpec((nb, hist, ch), lambda b, c: (b, 0, 0)),
                  pl.BlockSpec((width, ch), lambda b, c: (0, 0)),
                  pl.BlockSpec((1, ch), lambda b, c: (0, 0)),
                  pl.BlockSpec((1, ch), lambda b, c: (0, 0)),
                  pl.BlockSpec((1, ch), lambda b, c: (0, 0)),
                  pl.BlockSpec(memory_space=pl.ANY)],
        out_specs=[pl.BlockSpec((rows, ch), lambda b, c: (rb0 + b * nc + c, 1)),
                   pl.BlockSpec((nb, hist, ch), lambda b, c: (b, 0, 0))],
        out_shape=[jax.ShapeDtypeStruct(mixed.shape, mixed.dtype),
                   jax.ShapeDtypeStruct((batch, hist, ch), F32)],
        scratch_shapes=[pltpu.VMEM((nb, pad + hist + lc, ch), F32), pltpu.VMEM((rows, ch), F32),
                        pltpu.VMEM((8 * ((width - 1) // 8) + row_tile, ch_tile), F32)],
        input_output_aliases={6: 0},
        compiler_params=_params(("arbitrary", "arbitrary")),
        name="conformer_conv",
    )(u, state, conv_w, vec(conv_b), vec(ln_g), vec(ln_b), mixed)


HIGH_HALF = -(1 << 16)


def _pack_bf16_pairs(x):
    half = x.shape[1] // 2
    lo = lax.bitcast_convert_type(x[:, :half].astype(BF16).astype(F32), jnp.int32)
    hi = lax.bitcast_convert_type(x[:, half:].astype(BF16).astype(F32), jnp.int32)
    return lax.shift_right_logical(lo, 16) | (hi & HIGH_HALF)


def _unpack_bf16_pairs(w):
    lo = lax.bitcast_convert_type(lax.shift_left(w, 16), F32).astype(BF16)
    hi = lax.bitcast_convert_type(w & HIGH_HALF, F32).astype(BF16)
    return lo, hi


def _router_body(h_ref, g_ref, w_ref, b_ref, xn_ref, code_ref, gate_ref, cnt_ref, carry_ref,
                 *, n_experts):
    i = pl.program_id(0)

    @pl.when(i == 0)
    def _():
        carry_ref[...] = jnp.zeros(carry_ref.shape, F32)

    h = h_ref[...]
    xn = h * lax.rsqrt(jnp.mean(h * h, axis=-1, keepdims=True) + EPS) * g_ref[...]
    tm = h.shape[0]
    packed = _pack_bf16_pairs(xn)
    n_piece = packed.shape[1] // LANES
    for c in range(n_piece):
        xn_ref[pl.ds(c, tm, stride=n_piece), :] = packed[:, c * LANES:(c + 1) * LANES]
    x_hi = xn.astype(BF16)
    x_lo = (xn - x_hi.astype(F32)).astype(BF16)
    w = w_ref[...]
    w_hi = w.astype(BF16)
    w_lo = (w - w_hi.astype(F32)).astype(BF16)
    logits = (jnp.dot(x_hi, w_hi, preferred_element_type=F32)
              + jnp.dot(x_lo, w_hi, preferred_element_type=F32)
              + jnp.dot(x_hi, w_lo, preferred_element_type=F32)) + b_ref[...]
    lane = lax.broadcasted_iota(jnp.int32, (tm, LANES), 1)
    lane_f = lane.astype(F32)
    work = jnp.where(lane < n_experts, logits, -jnp.inf)
    tops, hots = [], []
    for _ in range(TOP_K):
        m = jnp.max(work, axis=-1, keepdims=True)
        idx = jnp.min(jnp.where(work == m, lane_f, float(LANES)), axis=-1, keepdims=True)
        hot = lane_f == idx
        tops.append(m)
        hots.append(hot)
        work = jnp.where(hot, -jnp.inf, work)
    exps = [jnp.exp(t - tops[0]) for t in tops]
    denom = sum(exps)
    chosen = sum(hot.astype(F32) for hot in hots)
    rows = lax.broadcasted_iota(jnp.int32, (tm, tm), 0)
    cols = lax.broadcasted_iota(jnp.int32, (tm, tm), 1)
    before = (rows > cols).astype(BF16)
    seen = jnp.dot(before, chosen.astype(BF16), preferred_element_type=F32) + carry_ref[0:1, :]
    code = jnp.zeros((tm, LANES), jnp.int32)
    gate = jnp.zeros((tm, LANES), F32)
    for kk in range(TOP_K):
        hot_f = hots[kk].astype(F32)
        rank = jnp.sum(seen * hot_f, axis=-1, keepdims=True)
        expert = jnp.sum(lane_f * hot_f, axis=-1, keepdims=True)
        packed = expert.astype(jnp.int32) * (1 << RANK_BITS) + rank.astype(jnp.int32)
        code = jnp.where(lane == kk, packed, code)
        gate = jnp.where(lane == kk, exps[kk] / denom, gate)
    code_ref[...] = code
    gate_ref[...] = gate
    total = carry_ref[...] + jnp.sum(chosen, axis=0, keepdims=True)
    carry_ref[...] = total
    cnt_ref[...] = total


def _router(h, ln_g, w_router, b_router):
    t, d = h.shape
    n_experts = w_router.shape[1]
    assert n_experts <= LANES and t < (1 << RANK_BITS)
    wr = jnp.zeros((d, LANES), F32).at[:, :n_experts].set(w_router)
    br = jnp.zeros((1, LANES), F32).at[0, :n_experts].set(b_router)
    tm = _tile(t, 256)
    assert (d // 2) % LANES == 0
    n_piece = d // 2 // LANES
    body = functools.partial(_router_body, n_experts=n_experts)
    return pl.pallas_call(
        body,
        grid=(t // tm,),
        in_specs=[pl.BlockSpec((tm, d), lambda i: (i, 0)),
                  pl.BlockSpec((1, d), lambda i: (0, 0)),
                  pl.BlockSpec((d, LANES), lambda i: (0, 0)),
                  pl.BlockSpec((1, LANES), lambda i: (0, 0))],
        out_specs=[pl.BlockSpec((tm * n_piece, LANES), lambda i: (i, 0)),
                   pl.BlockSpec((tm, LANES), lambda i: (i, 0)),
                   pl.BlockSpec((tm, LANES), lambda i: (i, 0)),
                   pl.BlockSpec((8, LANES), lambda i: (0, 0))],
        out_shape=[jax.ShapeDtypeStruct((t * n_piece, LANES), jnp.int32),
                   jax.ShapeDtypeStruct((t, LANES), jnp.int32),
                   jax.ShapeDtypeStruct((t, LANES), F32),
                   jax.ShapeDtypeStruct((8, LANES), F32)],
        scratch_shapes=[pltpu.VMEM((8, LANES), F32)],
        compiler_params=_params(("arbitrary",)),
        name="moe_router",
    )(h, ln_g.reshape(1, d), wr, br)


def _slot(code_ref, ps_ref, a):
    p = code_ref[a]
    return ps_ref[p >> RANK_BITS] + (p & ((1 << RANK_BITS) - 1))


def _dispatch_body(code_ref, ps_ref, cnt_ref, nu_ref, x_hbm, o_ref, tok_ref, buf_ref, sem,
                   *, n_tokens, n_experts, tm, n_piece):
    i = pl.program_id(0)
    nu = nu_ref[0]

    def row_copy(blk, r):
        half = blk % 2
        src = pl.multiple_of(tok_ref[blk * tm + r] * n_piece, n_piece)
        dst = pl.multiple_of(r * n_piece, n_piece)
        return pltpu.make_async_copy(x_hbm.at[pl.ds(src, n_piece), :],
                                     buf_ref.at[half, pl.ds(dst, n_piece), :], sem.at[half])

    def start_block(blk):
        def start(r8, carry):
            for lane in range(8):
                row_copy(blk, r8 * 8 + lane).start(priority=lane % 2)
            return carry
        lax.fori_loop(0, tm // 8, start, 0)

    @pl.when(i == 0)
    def _():
        def pad_expert(e, carry):
            first = ps_ref[e] + cnt_ref[e]
            last = ps_ref[e] + (cnt_ref[e] + tm - 1) // tm * tm

            def clear(s):
                tok_ref[s] = 0
                return s + 1
            lax.while_loop(lambda s: s < last, clear, first)
            return carry
        lax.fori_loop(0, n_experts, pad_expert, 0)

        def place(t):
            for kk in range(TOP_K):
                tok_ref[_slot(code_ref, ps_ref, t * TOP_K + kk)] = t
            return t + 1
        lax.while_loop(lambda t: t < n_tokens, place, 0)
        start_block(0)

    @pl.when(i + 1 < nu)
    def _():
        start_block(i + 1)

    @pl.when(i < nu)
    def _():
        def wait(r, carry):
            row_copy(i, r).wait()
            return carry
        lax.fori_loop(0, tm, wait, 0, unroll=8)
        half_d = n_piece * LANES
        for c in range(n_piece):
            lo, hi = _unpack_bf16_pairs(buf_ref[i % 2, pl.ds(c, tm, stride=n_piece), :])
            o_ref[:, c * LANES:(c + 1) * LANES] = lo
            o_ref[:, half_d + c * LANES:half_d + (c + 1) * LANES] = hi

    @pl.when(i >= nu)
    def _():
        o_ref[...] = jnp.zeros(o_ref.shape, o_ref.dtype)


def _dispatch(xn_packed, code, pad_starts, counts, nu, *, d, n_blocks, tm):
    n_piece = d // 2 // LANES
    t = xn_packed.shape[0] // n_piece
    body = functools.partial(_dispatch_body, n_tokens=t, n_experts=counts.shape[0], tm=tm, n_piece=n_piece)
    return pl.pallas_call(
        body,
        grid_spec=pltpu.PrefetchScalarGridSpec(
            num_scalar_prefetch=4,
            grid=(n_blocks,),
            in_specs=[pl.BlockSpec(memory_space=pl.ANY)],
            out_specs=pl.BlockSpec((tm, d), lambda i, c_r, p_r, n_r, nu_r: (i, 0)),
            scratch_shapes=[pltpu.SMEM((n_blocks * tm,), jnp.int32),
                            pltpu.VMEM((2, tm * n_piece, LANES), jnp.int32),
                            pltpu.SemaphoreType.DMA((2,))]),
        out_shape=jax.ShapeDtypeStruct((n_blocks * tm, d), BF16),
        compiler_params=_params(("arbitrary",)),
        name="moe_dispatch",
    )(code, pad_starts, counts, nu, xn_packed)


def _combine_body(code_ref, ps_ref, y_hbm, h_ref, gate_ref, g_ref, h_out_ref, xn_ref, buf_ref, sem, *, tm):
    i = pl.program_id(0)

    def row_copy(blk, r, kk):
        half = blk % 2
        slot = _slot(code_ref, ps_ref, (blk * tm + r) * TOP_K + kk)
        return pltpu.make_async_copy(y_hbm.at[pl.ds(slot, 1), :],
                                     buf_ref.at[half, kk, pl.ds(r, 1), :], sem.at[half])

    def start_block(blk):
        def start(r, carry):
            for kk in range(TOP_K):
                row_copy(blk, r, kk).start(priority=kk % 2)
            return carry
        lax.fori_loop(0, tm, start, 0, unroll=2)

    @pl.when(i == 0)
    def _():
        start_block(0)

    @pl.when(i + 1 < pl.num_programs(0))
    def _():
        start_block(i + 1)

    def wait(r, carry):
        for kk in range(TOP_K):
            row_copy(i, r, kk).wait()
        return carry
    lax.fori_loop(0, tm, wait, 0, unroll=2)

    gate = gate_ref[...]
    h = h_ref[...]
    for kk in range(TOP_K):
        h = h + gate[:, kk:kk + 1] * buf_ref[i % 2, kk]
    h_out_ref[...] = h
    xn = h * lax.rsqrt(jnp.mean(h * h, axis=-1, keepdims=True) + EPS) * g_ref[...]
    xn_ref[...] = xn.astype(xn_ref.dtype)


def _combine(yb, h, gate, code, pad_starts, ln_g):
    t, d = h.shape
    tm = _tile(t, 128)
    body = functools.partial(_combine_body, tm=tm)
    return pl.pallas_call(
        body,
        grid_spec=pltpu.PrefetchScalarGridSpec(
            num_scalar_prefetch=2,
            grid=(t // tm,),
            in_specs=[pl.BlockSpec(memory_space=pl.ANY),
                      pl.BlockSpec((tm, d), lambda i, c_r, p_r: (i, 0)),
                      pl.BlockSpec((tm, LANES), lambda i, c_r, p_r: (i, 0)),
                      pl.BlockSpec((1, d), lambda i, c_r, p_r: (0, 0))],
            out_specs=[pl.BlockSpec((tm, d), lambda i, c_r, p_r: (i, 0)),
                       pl.BlockSpec((tm, d), lambda i, c_r, p_r: (i, 0))],
            scratch_shapes=[pltpu.VMEM((2, TOP_K, tm, d), F32),
                            pltpu.SemaphoreType.DMA((2,))]),
        out_shape=[jax.ShapeDtypeStruct((t, d), F32),
                   jax.ShapeDtypeStruct((t, d), BF16)],
        compiler_params=_params(("arbitrary",)),
        name="moe_combine",
    )(code, pad_starts, yb, h, gate, ln_g.reshape(1, d))


def _layer(x, p, groups, s_gla, s_conv, ln_mix, w_in, w_gate_lr, b_gate, gla_norm, conv_w, conv_b,
           conv_ln_g, conv_ln_b, w_out, ln_moe, w_router, b_router, w_gu, b_gu, w_dn, b_dn,
           ln_ple, w_ple_gate, w_ple_proj):
    t, d = x.shape
    heads, dk, dv = s_gla[0].shape[1:]
    qk, d_gla = heads * dk, heads * dv
    rank = w_gate_lr.shape[0]
    d_conv = conv_w.shape[1]
    n_experts, _, d_ff2 = w_gu.shape
    d_ff = d_ff2 // 2
    n_qkvr = 2 * qk + 2 * d_gla
    glu0 = n_qkvr + rank
    tm_dense = _tile(t, 1024)
    tn = 512

    xn1 = _rmsnorm(x, ln_mix, BF16)
    qkvr = _dense(xn1, w_in[:, :n_qkvr], [0], tm=tm_dense, tn=_tile(n_qkvr, tn, LANES), n_out=n_qkvr, out_dtype=F32,
                  epilogue=_ep_plain, name="in_proj_qkvr")
    u = _dense(xn1, w_in[:, glu0:], [0, d_conv], tm=tm_dense, tn=_tile(d_conv, tn // 2, LANES), n_out=d_conv,
               out_dtype=F32, epilogue=_ep_glu, name="in_proj_glu")
    log_a = _log_decay(xn1, w_in[:, n_qkvr:glu0], w_gate_lr, b_gate)
    new_gla, new_conv = [], []
    assert d_gla == d_conv
    mixed = jnp.zeros((t, d_gla + d_conv), BF16)
    for (row0, batch, length), sg, sc in zip(groups, s_gla, s_conv):
        mixed, s_g = _gla(qkvr, log_a, sg, gla_norm, mixed, row0=row0, batch=batch, length=length)
        mixed, s_c = _conv(u, sc, conv_w, conv_b, conv_ln_g, conv_ln_b, mixed, row0=row0, batch=batch,
                           length=length)
        new_gla.append(s_g)
        new_conv.append(s_c)
    h1 = _dense(mixed, w_out, [0], tm=tm_dense, tn=_tile(d, tn, LANES), n_out=d, out_dtype=F32,
                epilogue=_ep_residual, extras=[(x, "tile")], name="out_proj")

    xn2, code, gate, counts = _router(h1, ln_moe, w_router, b_router)
    tm_e = _tile(t, EXPERT_ROWS)
    counts = counts[0, :n_experts].astype(jnp.int32)
    padded = (counts + tm_e - 1) // tm_e * tm_e
    pad_ends = jnp.cumsum(padded)
    pad_starts = (pad_ends - padded).astype(jnp.int32)
    n_blocks = -(-t * TOP_K // tm_e) + n_experts
    nu = (pad_ends[-1:] // tm_e).astype(jnp.int32)
    groups = _expert_groups(counts, pad_starts, sb=tm_e, n_sub_max=GROUP_BLOCKS, n_blocks=n_blocks)
    code = code[:, :TOP_K].reshape(-1)
    xb = _dispatch(xn2, code, pad_starts, counts, nu, d=d, n_blocks=n_blocks, tm=tm_e)
    act = _emm(xb, w_gu, [0, d_ff], b_gu.reshape(n_experts, 1, d_ff2), groups, nu, sb=tm_e,
               n_sub_max=GROUP_BLOCKS, tn=_tile(d_ff, tn, LANES), n_out=d_ff, out_dtype=BF16,
               epilogue=_ep_swiglu, name="moe_gate_up")
    yb = _emm(act, w_dn, [0], b_dn.reshape(n_experts, 1, d), groups, nu, sb=tm_e,
              n_sub_max=GROUP_BLOCKS, tn=_tile(d, 2 * tn, LANES), n_out=d, out_dtype=F32,
              epilogue=_ep_bias, name="moe_down")
    h2, xn3 = _combine(yb, h1, gate, code, pad_starts, ln_ple)

    h3 = _dense(xn3, w_ple_gate, [0], tm=_tile(t, 512), tn=_tile(d, tn, LANES), n_out=d, out_dtype=F32,
                epilogue=_ep_ple, extras=[(h2, "tile"), (p.astype(BF16), "row"), (w_ple_proj, "col")],
                name="ple")
    return h3, new_gla, new_conv


def kernel(x_prompt, x_sample, p_prompt, p_sample, state_gla, state_conv, ln_mix, w_in, w_gate_lr, b_gate, gla_norm, conv_w, conv_b, conv_ln_g, conv_ln_b, w_out, ln_moe, w_router, b_router, w_gu, b_gu, w_dn, b_dn, ln_ple, w_ple_gate, w_ple_proj, ln_final):
    bp, lp, d = x_prompt.shape
    bs, ls, _ = x_sample.shape
    depth = state_gla.shape[0]
    tp, ts = bp * lp, bs * ls
    groups = [(0, bp, lp), (tp, bs, ls)]
    h = jnp.concatenate([x_prompt.reshape(tp, d), x_sample.reshape(ts, d)], axis=0)
    zero_gla = jnp.zeros((bp,) + state_gla.shape[2:], state_gla.dtype)
    zero_conv = jnp.zeros((bp,) + state_conv.shape[2:], state_conv.dtype)
    gla_p, gla_s, conv_p, conv_s = [], [], [], []
    for i in range(depth):
        p = jnp.concatenate([p_prompt[i].reshape(tp, -1), p_sample[i].reshape(ts, -1)], axis=0)
        h, new_gla, new_conv = _layer(
            h, p, groups, [zero_gla, state_gla[i]], [zero_conv, state_conv[i]], ln_mix[i], w_in[i],
            w_gate_lr[i], b_gate[i], gla_norm[i], conv_w[i], conv_b[i], conv_ln_g[i], conv_ln_b[i],
            w_out[i], ln_moe[i], w_router[i], b_router[i], w_gu[i], b_gu[i], w_dn[i], b_dn[i],
            ln_ple[i], w_ple_gate[i], w_ple_proj[i])
        gla_p.append(new_gla[0])
        gla_s.append(new_gla[1])
        conv_p.append(new_conv[0])
        conv_s.append(new_conv[1])
    y_prompt = _rmsnorm(h, ln_final, F32, row0=0, rows=tp).reshape(bp, lp, d)
    y_sample = _rmsnorm(h, ln_final, F32, row0=tp, rows=ts).reshape(bs, ls, d)
    return (y_prompt, y_sample, jnp.stack(gla_p, 0), jnp.stack(conv_p, 0),
            jnp.stack(gla_s, 0), jnp.stack(conv_s, 0))
```

```python
import functools

import jax
import jax.numpy as jnp
from jax import lax
from jax.experimental import pallas as pl
from jax.experimental.pallas import tpu as pltpu

EPS = 1e-6
TOP_K = 4
GATE_NORMALIZER = 16.0
GLA_CHUNK = 16
SWIGLU_LIMIT = 7.0
SWIGLU_ALPHA = 1.702

LANES = 128
VMEM_LIMIT_BYTES = 60 * 2**20
VMEM_BUDGET_BYTES = 54 * 2**20
RANK_BITS = 16
EXPERT_ROWS = 256
GROUP_BLOCKS = 5

F32 = jnp.float32
BF16 = jnp.bfloat16


def _tile(n, pref, align=8):
    if n <= pref:
        return n
    for t in range(pref, 0, -1):
        if n % t == 0 and t % align == 0:
            return t
    return n


def _params(sem):
    return pltpu.CompilerParams(dimension_semantics=sem, vmem_limit_bytes=VMEM_LIMIT_BYTES)


def _rms_body(x_ref, g_ref, o_ref):
    x = x_ref[...]
    y = x * lax.rsqrt(jnp.mean(x * x, axis=-1, keepdims=True) + EPS)
    o_ref[...] = (y * g_ref[...]).astype(o_ref.dtype)


def _rmsnorm(x, g, out_dtype, row0=0, rows=None):
    rows = x.shape[0] if rows is None else rows
    d = x.shape[1]
    tm = _tile(rows, 512)
    assert row0 % tm == 0
    off = row0 // tm
    return pl.pallas_call(
        _rms_body,
        grid=(rows // tm,),
        in_specs=[pl.BlockSpec((tm, d), lambda i: (i + off, 0)),
                  pl.BlockSpec((1, d), lambda i: (0, 0))],
        out_specs=pl.BlockSpec((tm, d), lambda i: (i, 0)),
        out_shape=jax.ShapeDtypeStruct((rows, d), out_dtype),
        compiler_params=_params(("arbitrary",)),
        name="rmsnorm",
    )(x, g.reshape(1, d))


def _gmm_body(be_ref, nu_ref, *refs, n_w, n_b, n_extra, epilogue, cast_rows):
    x_ref = refs[0]
    w_refs = refs[1:1 + n_w]
    b_refs = refs[1 + n_w:1 + n_w + n_b]
    e_refs = refs[1 + n_w + n_b:1 + n_w + n_b + n_extra]
    o_ref = refs[1 + n_w + n_b + n_extra]
    wb_refs = refs[2 + n_w + n_b + n_extra:]
    i = pl.program_id(1)
    valid = i < nu_ref[0]
    prev = jnp.maximum(i - 1, 0)
    changed = jnp.logical_or(i == 0, be_ref[i] != be_ref[prev])
    k = w_refs[0].shape[0]

    @pl.when(jnp.logical_and(valid, changed))
    def _():
        def cast(r, carry):
            sl = pl.ds(pl.multiple_of(r * cast_rows, cast_rows), cast_rows)
            for w_ref, wb_ref in zip(w_refs, wb_refs):
                wb_ref[sl, :] = w_ref[sl, :].astype(BF16)
            return carry
        lax.fori_loop(0, k // cast_rows, cast, 0)

    @pl.when(valid)
    def _():
        x = x_ref[...]
        accs = [jnp.dot(x, wb_ref[...], preferred_element_type=F32) for wb_ref in wb_refs]
        o_ref[...] = epilogue(accs, [b[...] for b in b_refs], [e[...] for e in e_refs]).astype(o_ref.dtype)

    @pl.when(jnp.logical_not(valid))
    def _():
        o_ref[...] = jnp.zeros(o_ref.shape, o_ref.dtype)


def _gmm(x, w, col_offsets, bias, be, nu, *, tm, tn, n_out, out_dtype, epilogue, extras=(), name):
    r, k = x.shape
    nb = r // tm
    assert r % tm == 0 and n_out % tn == 0 and all(o % tn == 0 for o in col_offsets)
    n_w = len(col_offsets)

    def row(i, nu_ref):
        return jnp.minimum(i, nu_ref[0] - 1)

    in_specs = [pl.BlockSpec((tm, k), lambda j, i, be_r, nu_r: (row(i, nu_r), 0))]
    args = [x]
    for off in col_offsets:
        ob = off // tn
        in_specs.append(pl.BlockSpec((None, k, tn),
                                     lambda j, i, be_r, nu_r, ob=ob: (be_r[row(i, nu_r)], 0, j + ob)))
        args.append(w)
    n_b = 0
    if bias is not None:
        for off in col_offsets:
            ob = off // tn
            in_specs.append(pl.BlockSpec((None, 1, tn),
                                         lambda j, i, be_r, nu_r, ob=ob: (be_r[row(i, nu_r)], 0, j + ob)))
            args.append(bias)
            n_b += 1
    for arr, kind in extras:
        if kind == "tile":
            in_specs.append(pl.BlockSpec((tm, tn), lambda j, i, be_r, nu_r: (row(i, nu_r), j)))
        elif kind == "row":
            in_specs.append(pl.BlockSpec((tm, arr.shape[1]), lambda j, i, be_r, nu_r: (row(i, nu_r), 0)))
        else:
            in_specs.append(pl.BlockSpec((arr.shape[0], tn), lambda j, i, be_r, nu_r: (0, j)))
        args.append(arr)
    vmem = (2 * tm * k * 2 + n_w * (2 * k * tn * 4 + k * tn * 2) + n_w * tm * tn * 4
            + 2 * tm * tn * jnp.dtype(out_dtype).itemsize)
    for arr, kind in extras:
        blk = {"tile": tm * tn, "row": tm * arr.shape[1], "col": arr.shape[0] * tn}[kind]
        vmem += 2 * blk * arr.dtype.itemsize
    assert vmem <= VMEM_BUDGET_BYTES, (name, vmem)
    cast_rows = _tile(k, 256)
    body = functools.partial(_gmm_body, n_w=n_w, n_b=n_b, n_extra=len(extras), epilogue=epilogue,
                             cast_rows=cast_rows)
    return pl.pallas_call(
        body,
        grid_spec=pltpu.PrefetchScalarGridSpec(
            num_scalar_prefetch=2,
            grid=(n_out // tn, nb),
            in_specs=in_specs,
            out_specs=pl.BlockSpec((tm, tn), lambda j, i, be_r, nu_r: (i, j)),
            scratch_shapes=[pltpu.VMEM((k, tn), BF16) for _ in range(n_w)]),
        out_shape=jax.ShapeDtypeStruct((r, n_out), out_dtype),
        compiler_params=_params(("arbitrary", "arbitrary")),
        name=name,
    )(be, nu, *args)


def _dense(x, w2d, col_offsets, *, tm, tn, n_out, out_dtype, epilogue, extras=(), name):
    nb = x.shape[0] // tm
    return _gmm(x, w2d[None], col_offsets, None, jnp.zeros((nb,), jnp.int32),
                jnp.full((1,), nb, jnp.int32), tm=tm, tn=tn, n_out=n_out, out_dtype=out_dtype,
                epilogue=epilogue, extras=extras, name=name)


def _emm_body(ge_ref, gb_ref, ns_ref, gs_ref, nu_ref, x_hbm, *refs, n_w, n_b, epilogue, sb, tn, n_blocks,
              cast_rows):
    w_refs = refs[:n_w]
    b_refs = refs[n_w:n_w + n_b]
    o_hbm = refs[n_w + n_b]
    xbuf, stage, pend, xsem, osem = refs[n_w + n_b + 1:n_w + n_b + 6]
    wb_refs = refs[n_w + n_b + 6:]
    g = pl.program_id(0)
    n = pl.program_id(1)
    nsub = ns_ref[g]
    blk0 = gb_ref[g]
    k = x_hbm.shape[1]
    n_col = o_hbm.shape[1] // tn

    n_groups = pl.num_programs(0)
    last_col = n == pl.num_programs(1) - 1
    g_next = jnp.minimum(g + 1, n_groups - 1)
    nsub_next = jnp.where(g + 1 < n_groups, ns_ref[g_next], 0)
    blk0_next = gb_ref[g_next]

    def x_copy(first_blk, s):
        row = pl.multiple_of((first_blk + s) * sb, sb)
        return pltpu.make_async_copy(x_hbm.at[pl.ds(row, sb), :], xbuf.at[s], xsem)

    def out_copy(half, row, col):
        return pltpu.make_async_copy(stage.at[half],
                                     o_hbm.at[pl.ds(pl.multiple_of(row, sb), sb),
                                              pl.ds(pl.multiple_of(col, tn), tn)], osem.at[half])

    def wait_half(half):
        @pl.when(pend[3 * half] == 1)
        def _():
            out_copy(half, pend[3 * half + 1], pend[3 * half + 2]).wait()
            pend[3 * half] = 0

    @pl.when(jnp.logical_and(g == 0, n == 0))
    def _():
        pend[0] = 0
        pend[3] = 0

        def start(s, carry):
            x_copy(blk0, s).start()
            return carry
        lax.fori_loop(0, nsub, start, 0)

    @pl.when(n == 0)
    def _():
        def wait(s, carry):
            x_copy(blk0, s).wait()
            return carry
        lax.fori_loop(0, nsub, wait, 0)

    @pl.when(nsub > 0)
    def _():
        def cast(r, carry):
            sl = pl.ds(pl.multiple_of(r * cast_rows, cast_rows), cast_rows)
            for w_ref, wb_ref in zip(w_refs, wb_refs):
                wb_ref[sl, :] = w_ref[sl, :].astype(BF16)
            return carry
        lax.fori_loop(0, k // cast_rows, cast, 0)

    def block(s, rows):
        half = s % 2
        x = xbuf[s] if rows == sb else xbuf[s, pl.ds(0, rows), :]
        accs = [jnp.dot(x, wb_ref[...], preferred_element_type=F32) for wb_ref in wb_refs]
        res = epilogue(accs, [b[...] for b in b_refs], []).astype(stage.dtype)
        wait_half(half)
        if rows == sb:
            stage[half] = res
        else:
            stage[half, pl.ds(0, rows), :] = res
            stage[half, pl.ds(rows, sb - rows), :] = jnp.zeros((sb - rows, tn), stage.dtype)
        row = (blk0 + s) * sb
        col = n * tn
        out_copy(half, row, col).start()
        pend[3 * half] = 1
        pend[3 * half + 1] = row
        pend[3 * half + 2] = col

        @pl.when(jnp.logical_and(last_col, s < nsub_next))
        def _():
            x_copy(blk0_next, s).start()

    short = gs_ref[g]

    def sub(s, carry):
        block(s, sb)
        return carry
    lax.fori_loop(0, nsub - short, sub, 0)

    @pl.when(short == 1)
    def _():
        block(nsub - 1, sb // 2)

    @pl.when(last_col)
    def _():
        def start_rest(s, carry):
            x_copy(blk0_next, s).start()
            return carry
        lax.fori_loop(nsub, nsub_next, start_rest, 0)

    @pl.when(jnp.logical_and(g == pl.num_programs(0) - 1, n == pl.num_programs(1) - 1))
    def _():
        wait_half(0)
        wait_half(1)
        stage[0] = jnp.zeros(stage.shape[1:], stage.dtype)

        def zero_start(blk, carry):
            for c in range(n_col):
                out_copy(0, blk * sb, c * tn).start()
            return carry
        lax.fori_loop(nu_ref[0], n_blocks, zero_start, 0)

        def zero_wait(blk, carry):
            for c in range(n_col):
                out_copy(0, blk * sb, c * tn).wait()
            return carry
        lax.fori_loop(nu_ref[0], n_blocks, zero_wait, 0)


def _emm(x, w, col_offsets, bias, groups, nu, *, sb, n_sub_max, tn, n_out, out_dtype, epilogue, name):
    r, k = x.shape
    ge, gb, ns, gs = groups
    n_groups = ge.shape[0]
    n_col = n_out // tn
    n_w = len(col_offsets)
    assert r % sb == 0 and n_out % tn == 0 and all(o % tn == 0 for o in col_offsets)

    def wmap(ob):
        def index(g, n, ge_r, gb_r, ns_r, gs_r, nu_r):
            return ge_r[g], 0, jnp.where(ns_r[g] > 0, n, n_col - 1) + ob
        return index

    in_specs = [pl.BlockSpec(memory_space=pl.ANY)]
    args = [x]
    for off in col_offsets:
        in_specs.append(pl.BlockSpec((None, k, tn), wmap(off // tn)))
        args.append(w)
    for off in col_offsets:
        in_specs.append(pl.BlockSpec((None, 1, tn), wmap(off // tn)))
        args.append(bias)
    out_bytes = jnp.dtype(out_dtype).itemsize
    vmem = (n_sub_max * sb * k * 2 + n_w * (2 * k * tn * 4 + k * tn * 2) + n_w * sb * tn * 4
            + 2 * sb * tn * out_bytes)
    assert vmem <= VMEM_BUDGET_BYTES, (name, vmem)
    body = functools.partial(_emm_body, n_w=n_w, n_b=n_w, epilogue=epilogue, sb=sb, tn=tn,
                             n_blocks=r // sb, cast_rows=_tile(k, 256))
    return pl.pallas_call(
        body,
        grid_spec=pltpu.PrefetchScalarGridSpec(
            num_scalar_prefetch=5,
            grid=(n_groups, n_col),
            in_specs=in_specs,
            out_specs=pl.BlockSpec(memory_space=pl.ANY),
            scratch_shapes=[pltpu.VMEM((n_sub_max, sb, k), BF16),
                            pltpu.VMEM((2, sb, tn), out_dtype),
                            pltpu.SMEM((8,), jnp.int32),
                            pltpu.SemaphoreType.DMA(()),
                            pltpu.SemaphoreType.DMA((2,))]
                           + [pltpu.VMEM((k, tn), BF16) for _ in range(n_w)]),
        out_shape=jax.ShapeDtypeStruct((r, n_out), out_dtype),
        compiler_params=_params(("arbitrary", "arbitrary")),
        name=name,
    )(ge, gb, ns, gs, nu, *args)


def _expert_groups(counts, pad_starts, *, sb, n_sub_max, n_blocks):
    n_experts = counts.shape[0]
    nblk = (counts + sb - 1) // sb
    ngrp = (nblk + n_sub_max - 1) // n_sub_max
    gend = jnp.cumsum(ngrp)
    gstart = gend - ngrp
    n_groups = n_experts + -(-n_blocks // n_sub_max)
    gid = jnp.arange(n_groups, dtype=jnp.int32)
    valid = gid < gend[-1]
    last = jnp.maximum(gend[-1] - 1, 0)
    ge = jnp.sum(gend[None, :] <= jnp.where(valid, gid, last)[:, None], axis=1)
    ge = jnp.minimum(ge, n_experts - 1).astype(jnp.int32)
    j = gid - gstart[ge]
    gb = pad_starts[ge] // sb + j * n_sub_max
    ns = jnp.where(valid, jnp.clip(nblk[ge] - j * n_sub_max, 0, n_sub_max), 0)
    tail = counts[ge] - (nblk[ge] - 1) * sb
    gs = valid & (j == ngrp[ge] - 1) & (tail <= sb // 2) & (ns > 0)
    return ge, gb.astype(jnp.int32), ns.astype(jnp.int32), gs.astype(jnp.int32)


def _sigmoid(x):
    return 1.0 / (1.0 + jnp.exp(-x))


def _ep_plain(accs, biases, extras):
    return accs[0]


def _ep_glu(accs, biases, extras):
    return accs[0] * _sigmoid(accs[1])


def _ep_residual(accs, biases, extras):
    return accs[0] + extras[0]


def _ep_swiglu(accs, biases, extras):
    g = jnp.minimum(accs[0] + biases[0], SWIGLU_LIMIT)
    up = jnp.clip(accs[1] + biases[1], -SWIGLU_LIMIT, SWIGLU_LIMIT)
    return g * _sigmoid(SWIGLU_ALPHA * g) * (up + 1.0)


def _ep_bias(accs, biases, extras):
    return accs[0] + biases[0]


def _ep_ple(accs, biases, extras):
    h, p, wp = extras
    pp = jnp.dot(p, wp.astype(BF16), preferred_element_type=F32)
    return h + _sigmoid(accs[0]) * pp


def _loga_body(x_ref, wa_ref, wg_ref, bg_ref, o_ref):
    a = jnp.dot(x_ref[...], wa_ref[...], preferred_element_type=F32)
    z = jnp.dot(a.astype(BF16), wg_ref[...], preferred_element_type=F32) + bg_ref[...]
    log_sig = jnp.minimum(z, 0.0) - jnp.log1p(jnp.exp(-jnp.abs(z)))
    o_ref[...] = log_sig / GATE_NORMALIZER


def _log_decay(xn, w_alr, w_gate_lr, b_gate):
    t, d = xn.shape
    rank, qk = w_gate_lr.shape
    assert rank <= LANES
    wa = jnp.zeros((d, LANES), BF16).at[:, :rank].set(w_alr.astype(BF16))
    wg = jnp.zeros((LANES, qk), BF16).at[:rank, :].set(w_gate_lr.astype(BF16))
    tm = _tile(t, 512)
    return pl.pallas_call(
        _loga_body,
        grid=(t // tm,),
        in_specs=[pl.BlockSpec((tm, d), lambda i: (i, 0)),
                  pl.BlockSpec((d, LANES), lambda i: (0, 0)),
                  pl.BlockSpec((LANES, qk), lambda i: (0, 0)),
                  pl.BlockSpec((1, qk), lambda i: (0, 0))],
        out_specs=pl.BlockSpec((tm, qk), lambda i: (i, 0)),
        out_shape=jax.ShapeDtypeStruct((t, qk), F32),
        compiler_params=_params(("arbitrary",)),
        name="gla_log_decay",
    )(xn, wa, wg, b_gate.reshape(1, qk))


def _split3(x):
    hi = x.astype(BF16)
    r1 = x - hi.astype(F32)
    mid = r1.astype(BF16)
    lo = (r1 - mid.astype(F32)).astype(BF16)
    return hi, mid, lo


def _gla_body(q_ref, k_ref, v_ref, r_ref, la_ref, s0_ref, gn_ref, mix_in_ref, og_ref, sn_ref, s_ref,
              o_ref, *, chunk, n_sub, lc, scale):
    del mix_in_ref
    c = pl.program_id(1)

    @pl.when(c == 0)
    def _():
        s_ref[...] = s0_ref[...]

    nb, heads, dk, dv = s_ref.shape
    assert n_sub <= LANES
    rows = lax.broadcasted_iota(jnp.int32, (lc, lc), 0)
    cols = lax.broadcasted_iota(jnp.int32, (lc, lc), 1)
    same_chunk = (rows // chunk) == (cols // chunk)
    causal = jnp.logical_and(same_chunk, rows >= cols)
    tril = causal.astype(BF16)
    chunk_sum = same_chunk.astype(BF16)
    chunk_col = (lax.broadcasted_iota(jnp.int32, (lc, LANES), 0) // chunk
                 == lax.broadcasted_iota(jnp.int32, (lc, LANES), 1)).astype(BF16)
    tdims = (((0,), (0,)), ((), ()))

    chains = []
    for bb in range(nb):
        sl = slice(bb * lc, (bb + 1) * lc)
        for h in range(heads):
            kc = slice(h * dk, (h + 1) * dk)
            vc = slice(h * dv, (h + 1) * dv)
            q = q_ref[sl, kc] * scale
            k = k_ref[sl, kc]
            v = v_ref[sl, vc].astype(BF16)
            pieces = _split3(la_ref[sl, kc])
            b = sum(jnp.dot(tril, p, preferred_element_type=F32) for p in pieces)
            b_end = sum(jnp.dot(chunk_sum, p, preferred_element_type=F32) for p in pieces)
            b_end_col = sum(lax.dot_general(p, chunk_col, tdims, preferred_element_type=F32)
                            for p in pieces)
            decay = jnp.exp(b_end_col)
            qe = (q * jnp.exp(b)).astype(BF16)
            ke = (k * jnp.exp(-b)).astype(BF16)
            kd = (k * jnp.exp(b_end - b)).astype(BF16)
            att = lax.dot_general(qe, ke, (((1,), (1,)), ((), ())), preferred_element_type=F32)
            att = jnp.where(causal, att, 0.0).astype(BF16)
            o_intra = jnp.dot(att, v, preferred_element_type=F32)
            chains.append((bb, h, vc, qe, kd, v, decay, o_intra))

    for i in range(n_sub):
        cs = slice(i * chunk, (i + 1) * chunk)
        for bb, h, vc, qe, kd, v, decay, o_intra in chains:
            s = s_ref[bb, h]
            o = jnp.dot(qe[cs], s.astype(BF16), preferred_element_type=F32) + o_intra[cs]
            s_ref[bb, h] = (decay[:, i:i + 1] * s
                            + lax.dot_general(kd[cs], v[cs], tdims, preferred_element_type=F32))
            o_ref[bb * lc + i * chunk:bb * lc + (i + 1) * chunk, vc] = o

    o = o_ref[...]
    r = r_ref[...]
    for h in range(heads):
        vc = slice(h * dv, (h + 1) * dv)
        oh = o[:, vc]
        oh = oh * lax.rsqrt(jnp.mean(oh * oh, axis=-1, keepdims=True) + EPS) * gn_ref[:, vc]
        og_ref[:, vc] = (oh * (r[:, vc] * _sigmoid(r[:, vc]))).astype(og_ref.dtype)

    @pl.when(c == pl.num_programs(1) - 1)
    def _():
        sn_ref[...] = s_ref[...]


def _gla(qkvr, log_a, s0, gla_norm, mixed, *, row0, batch, length):
    _, heads, dk, dv = s0.shape
    qk, d_gla = heads * dk, heads * dv
    chunk = min(GLA_CHUNK, length)
    lc = _tile(length, 256, chunk)
    nc = length // lc
    nb = next(n for n in (2, 1) if batch % n == 0) if (nc == 1 and lc <= 16) else 1
    rows = nb * lc
    assert length % chunk == 0 and row0 % rows == 0 and (2 * qk) % d_gla == 0
    assert rows % 16 == 0
    rb0 = row0 // rows
    v_blk = 2 * qk // d_gla
    body = functools.partial(_gla_body, chunk=chunk, n_sub=lc // chunk, lc=lc, scale=dk ** -0.5)
    rowblk = lambda b, c: rb0 + b * nc + c
    og, s_new = pl.pallas_call(
        body,
        grid=(batch // nb, nc),
        in_specs=[pl.BlockSpec((rows, qk), lambda b, c: (rowblk(b, c), 0)),
                  pl.BlockSpec((rows, qk), lambda b, c: (rowblk(b, c), 1)),
                  pl.BlockSpec((rows, d_gla), lambda b, c: (rowblk(b, c), v_blk)),
                  pl.BlockSpec((rows, d_gla), lambda b, c: (rowblk(b, c), v_blk + 1)),
                  pl.BlockSpec((rows, qk), lambda b, c: (rowblk(b, c), 0)),
                  pl.BlockSpec((nb, heads, dk, dv), lambda b, c: (b, 0, 0, 0)),
                  pl.BlockSpec((1, d_gla), lambda b, c: (0, 0)),
                  pl.BlockSpec(memory_space=pl.ANY)],
        out_specs=[pl.BlockSpec((rows, d_gla), lambda b, c: (rowblk(b, c), 0)),
                   pl.BlockSpec((nb, heads, dk, dv), lambda b, c: (b, 0, 0, 0))],
        out_shape=[jax.ShapeDtypeStruct(mixed.shape, mixed.dtype),
                   jax.ShapeDtypeStruct((batch, heads, dk, dv), F32)],
        scratch_shapes=[pltpu.VMEM((nb, heads, dk, dv), F32), pltpu.VMEM((rows, d_gla), F32)],
        input_output_aliases={7: 0},
        compiler_params=_params(("arbitrary", "arbitrary")),
        name="gla",
    )(qkvr, qkvr, qkvr, qkvr, log_a, s0, gla_norm.reshape(1, d_gla), mixed)
    return og, s_new


def _conv_body(u_ref, st_ref, w_ref, b_ref, g_ref, beta_ref, mix_in_ref, c_ref, sn_ref, ext_ref, y_ref,
               sh_ref, *, width, lc, pad, row_tile, ch_tile):
    del mix_in_ref
    c = pl.program_id(1)
    hist = width - 1
    nb = st_ref.shape[0]
    ch = u_ref.shape[1]

    @pl.when(c == 0)
    def _():
        for bb in range(nb):
            if pad:
                ext_ref[bb, 0:pad, :] = jnp.zeros((pad, ch), F32)
            ext_ref[bb, pad:pad + hist, :] = st_ref[bb]

    base = pad + hist
    for bb in range(nb):
        ext_ref[bb, base:base + lc, :] = u_ref[bb * lc:(bb + 1) * lc, :]

    def taps(ci, carry):
        cs = pl.ds(pl.multiple_of(ci * ch_tile, ch_tile), ch_tile)
        for bb in range(nb):
            for r0 in range(0, lc, row_tile):
                acc = jnp.zeros((row_tile, ch_tile), F32) + b_ref[:, cs]
                for r in range(min(8, width)):
                    a_max = (width - 1 - r) // 8
                    start = pad + r + r0
                    span = 8 * a_max + row_tile
                    sh_ref[0:span, :] = ext_ref[bb, start:start + span, cs]
                    for a in range(a_max + 1):
                        j = 8 * a + r
                        acc = acc + w_ref[j:j + 1, cs] * sh_ref[8 * a:8 * a + row_tile, :]
                y_ref[bb * lc + r0:bb * lc + r0 + row_tile, cs] = acc
        return carry
    lax.fori_loop(0, ch // ch_tile, taps, 0)

    acc = y_ref[...]
    mu = jnp.mean(acc, axis=-1, keepdims=True)
    cen = acc - mu
    var = jnp.mean(cen * cen, axis=-1, keepdims=True)
    y = cen * lax.rsqrt(var + EPS) * g_ref[...] + beta_ref[...]
    c_ref[...] = (y * _sigmoid(y)).astype(c_ref.dtype)
    for bb in range(nb):
        tail = ext_ref[bb, pad + lc:pad + lc + hist, :]
        ext_ref[bb, pad:pad + hist, :] = tail

    @pl.when(c == pl.num_programs(1) - 1)
    def _():
        for bb in range(nb):
            sn_ref[bb] = ext_ref[bb, pad:pad + hist, :]


def _conv(u, state, conv_w, conv_b, ln_g, ln_b, mixed, *, row0, batch, length):
    ch = u.shape[1]
    width = conv_w.shape[0]
    hist = width - 1
    lc = _tile(length, 128)
    nc = length // lc
    nb = 2 if (nc == 1 and batch % 2 == 0) else 1
    rows = nb * lc
    assert row0 % rows == 0 and lc % 8 == 0 and mixed.shape[1] == 2 * ch
    rb0 = row0 // rows
    pad = (-hist) % 8
    row_tile = _tile(lc, 64)
    ch_tile = _tile(ch, 2 * LANES, LANES)
    body = functools.partial(_conv_body, width=width, lc=lc, pad=pad, row_tile=row_tile, ch_tile=ch_tile)
    vec = lambda a: a.reshape(1, ch)
    return pl.pallas_call(
        body,
        grid=(batch // nb, nc),
        in_specs=[pl.BlockSpec((rows, ch), lambda b, c: (rb0 + b * nc + c, 0)),
                  pl.BlockSpec((nb, hist, ch), lambda b, c: (b, 0, 0)),
                  pl.BlockSpec((width, ch), lambda b, c: (0, 0)),
                  pl.BlockSpec((1, ch), lambda b, c: (0, 0)),
                  pl.BlockSpec((1, ch), lambda b, c: (0, 0)),
                  pl.BlockSpec((1, ch), lambda b, c: (0, 0)),
                  pl.BlockSpec(memory_space=pl.ANY)],
        out_specs=[pl.BlockSpec((rows, ch), lambda b, c: (rb0 + b * nc + c, 1)),
                   pl.BlockSpec((nb, hist, ch), lambda b, c: (b, 0, 0))],
        out_shape=[jax.ShapeDtypeStruct(mixed.shape, mixed.dtype),
                   jax.ShapeDtypeStruct((batch, hist, ch), F32)],
        scratch_shapes=[pltpu.VMEM((nb, pad + hist + lc, ch), F32), pltpu.VMEM((rows, ch), F32),
                        pltpu.VMEM((8 * ((width - 1) // 8) + row_tile, ch_tile), F32)],
        input_output_aliases={6: 0},
        compiler_params=_params(("arbitrary", "arbitrary")),
        name="conformer_conv",
    )(u, state, conv_w, vec(conv_b), vec(ln_g), vec(ln_b), mixed)


HIGH_HALF = -(1 << 16)


def _pack_bf16_pairs(x):
    half = x.shape[1] // 2
    lo = lax.bitcast_convert_type(x[:, :half].astype(BF16).astype(F32), jnp.int32)
    hi = lax.bitcast_convert_type(x[:, half:].astype(BF16).astype(F32), jnp.int32)
    return lax.shift_right_logical(lo, 16) | (hi & HIGH_HALF)


def _unpack_bf16_pairs(w):
    lo = lax.bitcast_convert_type(lax.shift_left(w, 16), F32).astype(BF16)
    hi = lax.bitcast_convert_type(w & HIGH_HALF, F32).astype(BF16)
    return lo, hi


def _router_body(h_ref, g_ref, w_ref, b_ref, xn_ref, code_ref, gate_ref, cnt_ref, carry_ref,
                 *, n_experts):
    i = pl.program_id(0)

    @pl.when(i == 0)
    def _():
        carry_ref[...] = jnp.zeros(carry_ref.shape, F32)

    h = h_ref[...]
    xn = h * lax.rsqrt(jnp.mean(h * h, axis=-1, keepdims=True) + EPS) * g_ref[...]
    xn_ref[...] = _pack_bf16_pairs(xn)
    tm = h.shape[0]
    x_hi = xn.astype(BF16)
    x_lo = (xn - x_hi.astype(F32)).astype(BF16)
    w = w_ref[...]
    w_hi = w.astype(BF16)
    w_lo = (w - w_hi.astype(F32)).astype(BF16)
    logits = (jnp.dot(x_hi, w_hi, preferred_element_type=F32)
              + jnp.dot(x_lo, w_hi, preferred_element_type=F32)
              + jnp.dot(x_hi, w_lo, preferred_element_type=F32)) + b_ref[...]
    lane = lax.broadcasted_iota(jnp.int32, (tm, LANES), 1)
    lane_f = lane.astype(F32)
    work = jnp.where(lane < n_experts, logits, -jnp.inf)
    tops, hots = [], []
    for _ in range(TOP_K):
        m = jnp.max(work, axis=-1, keepdims=True)
        idx = jnp.min(jnp.where(work == m, lane_f, float(LANES)), axis=-1, keepdims=True)
        hot = lane_f == idx
        tops.append(m)
        hots.append(hot)
        work = jnp.where(hot, -jnp.inf, work)
    exps = [jnp.exp(t - tops[0]) for t in tops]
    denom = sum(exps)
    chosen = sum(hot.astype(F32) for hot in hots)
    rows = lax.broadcasted_iota(jnp.int32, (tm, tm), 0)
    cols = lax.broadcasted_iota(jnp.int32, (tm, tm), 1)
    before = (rows > cols).astype(BF16)
    seen = jnp.dot(before, chosen.astype(BF16), preferred_element_type=F32) + carry_ref[0:1, :]
    code = jnp.zeros((tm, LANES), jnp.int32)
    gate = jnp.zeros((tm, LANES), F32)
    for kk in range(TOP_K):
        hot_f = hots[kk].astype(F32)
        rank = jnp.sum(seen * hot_f, axis=-1, keepdims=True)
        expert = jnp.sum(lane_f * hot_f, axis=-1, keepdims=True)
        packed = expert.astype(jnp.int32) * (1 << RANK_BITS) + rank.astype(jnp.int32)
        code = jnp.where(lane == kk, packed, code)
        gate = jnp.where(lane == kk, exps[kk] / denom, gate)
    code_ref[...] = code
    gate_ref[...] = gate
    total = carry_ref[...] + jnp.sum(chosen, axis=0, keepdims=True)
    carry_ref[...] = total
    cnt_ref[...] = total


def _router(h, ln_g, w_router, b_router):
    t, d = h.shape
    n_experts = w_router.shape[1]
    assert n_experts <= LANES and t < (1 << RANK_BITS)
    wr = jnp.zeros((d, LANES), F32).at[:, :n_experts].set(w_router)
    br = jnp.zeros((1, LANES), F32).at[0, :n_experts].set(b_router)
    tm = _tile(t, 256)
    body = functools.partial(_router_body, n_experts=n_experts)
    return pl.pallas_call(
        body,
        grid=(t // tm,),
        in_specs=[pl.BlockSpec((tm, d), lambda i: (i, 0)),
                  pl.BlockSpec((1, d), lambda i: (0, 0)),
                  pl.BlockSpec((d, LANES), lambda i: (0, 0)),
                  pl.BlockSpec((1, LANES), lambda i: (0, 0))],
        out_specs=[pl.BlockSpec((tm, d // 2), lambda i: (i, 0)),
                   pl.BlockSpec((tm, LANES), lambda i: (i, 0)),
                   pl.BlockSpec((tm, LANES), lambda i: (i, 0)),
                   pl.BlockSpec((8, LANES), lambda i: (0, 0))],
        out_shape=[jax.ShapeDtypeStruct((t, d // 2), jnp.int32),
                   jax.ShapeDtypeStruct((t, LANES), jnp.int32),
                   jax.ShapeDtypeStruct((t, LANES), F32),
                   jax.ShapeDtypeStruct((8, LANES), F32)],
        scratch_shapes=[pltpu.VMEM((8, LANES), F32)],
        compiler_params=_params(("arbitrary",)),
        name="moe_router",
    )(h, ln_g.reshape(1, d), wr, br)


def _slot(code_ref, ps_ref, a):
    p = code_ref[a]
    return ps_ref[p >> RANK_BITS] + (p & ((1 << RANK_BITS) - 1))


def _dispatch_body(code_ref, ps_ref, cnt_ref, nu_ref, x_hbm, o_ref, tok_ref, buf_ref, sem,
                   *, n_tokens, n_experts, tm):
    i = pl.program_id(0)
    nu = nu_ref[0]

    def row_copy(blk, r):
        half = blk % 2
        return pltpu.make_async_copy(x_hbm.at[pl.ds(tok_ref[blk * tm + r], 1), :],
                                     buf_ref.at[half, pl.ds(r, 1), :], sem.at[half])

    def start_block(blk):
        def start(r8, carry):
            for lane in range(8):
                row_copy(blk, r8 * 8 + lane).start(priority=lane % 2)
            return carry
        lax.fori_loop(0, tm // 8, start, 0)

    @pl.when(i == 0)
    def _():
        def pad_expert(e, carry):
            first = ps_ref[e] + cnt_ref[e]
            last = ps_ref[e] + (cnt_ref[e] + tm - 1) // tm * tm

            def clear(s):
                tok_ref[s] = 0
                return s + 1
            lax.while_loop(lambda s: s < last, clear, first)
            return carry
        lax.fori_loop(0, n_experts, pad_expert, 0)

        def place(t):
            for kk in range(TOP_K):
                tok_ref[_slot(code_ref, ps_ref, t * TOP_K + kk)] = t
            return t + 1
        lax.while_loop(lambda t: t < n_tokens, place, 0)
        start_block(0)

    @pl.when(i + 1 < nu)
    def _():
        start_block(i + 1)

    @pl.when(i < nu)
    def _():
        def wait(r, carry):
            row_copy(i, r).wait()
            return carry
        lax.fori_loop(0, tm, wait, 0, unroll=8)
        half_d = buf_ref.shape[2]
        lo, hi = _unpack_bf16_pairs(buf_ref[i % 2])
        o_ref[:, :half_d] = lo
        o_ref[:, half_d:] = hi

    @pl.when(i >= nu)
    def _():
        o_ref[...] = jnp.zeros(o_ref.shape, o_ref.dtype)


def _dispatch(xn_packed, code, pad_starts, counts, nu, *, n_blocks, tm):
    t, half_d = xn_packed.shape
    d = 2 * half_d
    body = functools.partial(_dispatch_body, n_tokens=t, n_experts=counts.shape[0], tm=tm)
    return pl.pallas_call(
        body,
        grid_spec=pltpu.PrefetchScalarGridSpec(
            num_scalar_prefetch=4,
            grid=(n_blocks,),
            in_specs=[pl.BlockSpec(memory_space=pl.ANY)],
            out_specs=pl.BlockSpec((tm, d), lambda i, c_r, p_r, n_r, nu_r: (i, 0)),
            scratch_shapes=[pltpu.SMEM((n_blocks * tm,), jnp.int32),
                            pltpu.VMEM((2, tm, half_d), jnp.int32),
                            pltpu.SemaphoreType.DMA((2,))]),
        out_shape=jax.ShapeDtypeStruct((n_blocks * tm, d), BF16),
        compiler_params=_params(("arbitrary",)),
        name="moe_dispatch",
    )(code, pad_starts, counts, nu, xn_packed)


def _combine_body(code_ref, ps_ref, y_hbm, h_ref, gate_ref, g_ref, h_out_ref, xn_ref, buf_ref, sem, *, tm):
    i = pl.program_id(0)

    def row_copy(blk, r, kk):
        half = blk % 2
        slot = _slot(code_ref, ps_ref, (blk * tm + r) * TOP_K + kk)
        return pltpu.make_async_copy(y_hbm.at[pl.ds(slot, 1), :],
                                     buf_ref.at[half, kk, pl.ds(r, 1), :], sem.at[half])

    def start_block(blk):
        def start(r, carry):
            for kk in range(TOP_K):
                row_copy(blk, r, kk).start(priority=kk % 2)
            return carry
        lax.fori_loop(0, tm, start, 0, unroll=2)

    @pl.when(i == 0)
    def _():
        start_block(0)

    @pl.when(i + 1 < pl.num_programs(0))
    def _():
        start_block(i + 1)

    def wait(r, carry):
        for kk in range(TOP_K):
            row_copy(i, r, kk).wait()
        return carry
    lax.fori_loop(0, tm, wait, 0, unroll=2)

    gate = gate_ref[...]
    h = h_ref[...]
    for kk in range(TOP_K):
        h = h + gate[:, kk:kk + 1] * buf_ref[i % 2, kk]
    h_out_ref[...] = h
    xn = h * lax.rsqrt(jnp.mean(h * h, axis=-1, keepdims=True) + EPS) * g_ref[...]
    xn_ref[...] = xn.astype(xn_ref.dtype)


def _combine(yb, h, gate, code, pad_starts, ln_g):
    t, d = h.shape
    tm = _tile(t, 128)
    body = functools.partial(_combine_body, tm=tm)
    return pl.pallas_call(
        body,
        grid_spec=pltpu.PrefetchScalarGridSpec(
            num_scalar_prefetch=2,
            grid=(t // tm,),
            in_specs=[pl.BlockSpec(memory_space=pl.ANY),
                      pl.BlockSpec((tm, d), lambda i, c_r, p_r: (i, 0)),
                      pl.BlockSpec((tm, LANES), lambda i, c_r, p_r: (i, 0)),
                      pl.BlockSpec((1, d), lambda i, c_r, p_r: (0, 0))],
            out_specs=[pl.BlockSpec((tm, d), lambda i, c_r, p_r: (i, 0)),
                       pl.BlockSpec((tm, d), lambda i, c_r, p_r: (i, 0))],
            scratch_shapes=[pltpu.VMEM((2, TOP_K, tm, d), F32),
                            pltpu.SemaphoreType.DMA((2,))]),
        out_shape=[jax.ShapeDtypeStruct((t, d), F32),
                   jax.ShapeDtypeStruct((t, d), BF16)],
        compiler_params=_params(("arbitrary",)),
        name="moe_combine",
    )(code, pad_starts, yb, h, gate, ln_g.reshape(1, d))


def _layer(x, p, groups, s_gla, s_conv, ln_mix, w_in, w_gate_lr, b_gate, gla_norm, conv_w, conv_b,
           conv_ln_g, conv_ln_b, w_out, ln_moe, w_router, b_router, w_gu, b_gu, w_dn, b_dn,
           ln_ple, w_ple_gate, w_ple_proj):
    t, d = x.shape
    heads, dk, dv = s_gla[0].shape[1:]
    qk, d_gla = heads * dk, heads * dv
    rank = w_gate_lr.shape[0]
    d_conv = conv_w.shape[1]
    n_experts, _, d_ff2 = w_gu.shape
    d_ff = d_ff2 // 2
    n_qkvr = 2 * qk + 2 * d_gla
    glu0 = n_qkvr + rank
    tm_dense = _tile(t, 1024)
    tn = 512

    xn1 = _rmsnorm(x, ln_mix, BF16)
    qkvr = _dense(xn1, w_in[:, :n_qkvr], [0], tm=tm_dense, tn=_tile(n_qkvr, tn, LANES), n_out=n_qkvr, out_dtype=F32,
                  epilogue=_ep_plain, name="in_proj_qkvr")
    u = _dense(xn1, w_in[:, glu0:], [0, d_conv], tm=tm_dense, tn=_tile(d_conv, tn // 2, LANES), n_out=d_conv,
               out_dtype=F32, epilogue=_ep_glu, name="in_proj_glu")
    log_a = _log_decay(xn1, w_in[:, n_qkvr:glu0], w_gate_lr, b_gate)
    new_gla, new_conv = [], []
    assert d_gla == d_conv
    mixed = jnp.zeros((t, d_gla + d_conv), BF16)
    for (row0, batch, length), sg, sc in zip(groups, s_gla, s_conv):
        mixed, s_g = _gla(qkvr, log_a, sg, gla_norm, mixed, row0=row0, batch=batch, length=length)
        mixed, s_c = _conv(u, sc, conv_w, conv_b, conv_ln_g, conv_ln_b, mixed, row0=row0, batch=batch,
                           length=length)
        new_gla.append(s_g)
        new_conv.append(s_c)
    h1 = _dense(mixed, w_out, [0], tm=tm_dense, tn=_tile(d, tn, LANES), n_out=d, out_dtype=F32,
                epilogue=_ep_residual, extras=[(x, "tile")], name="out_proj")

    xn2, code, gate, counts = _router(h1, ln_moe, w_router, b_router)
    tm_e = _tile(t, EXPERT_ROWS)
    counts = counts[0, :n_experts].astype(jnp.int32)
    padded = (counts + tm_e - 1) // tm_e * tm_e
    pad_ends = jnp.cumsum(padded)
    pad_starts = (pad_ends - padded).astype(jnp.int32)
    n_blocks = -(-t * TOP_K // tm_e) + n_experts
    nu = (pad_ends[-1:] // tm_e).astype(jnp.int32)
    groups = _expert_groups(counts, pad_starts, sb=tm_e, n_sub_max=GROUP_BLOCKS, n_blocks=n_blocks)
    code = code[:, :TOP_K].reshape(-1)
    xb = _dispatch(xn2, code, pad_starts, counts, nu, n_blocks=n_blocks, tm=tm_e)
    act = _emm(xb, w_gu, [0, d_ff], b_gu.reshape(n_experts, 1, d_ff2), groups, nu, sb=tm_e,
               n_sub_max=GROUP_BLOCKS, tn=_tile(d_ff, tn, LANES), n_out=d_ff, out_dtype=BF16,
               epilogue=_ep_swiglu, name="moe_gate_up")
    yb = _emm(act, w_dn, [0], b_dn.reshape(n_experts, 1, d), groups, nu, sb=tm_e,
              n_sub_max=GROUP_BLOCKS, tn=_tile(d, 2 * tn, LANES), n_out=d, out_dtype=F32,
              epilogue=_ep_bias, name="moe_down")
    h2, xn3 = _combine(yb, h1, gate, code, pad_starts, ln_ple)

    h3 = _dense(xn3, w_ple_gate, [0], tm=_tile(t, 512), tn=_tile(d, tn, LANES), n_out=d, out_dtype=F32,
                epilogue=_ep_ple, extras=[(h2, "tile"), (p.astype(BF16), "row"), (w_ple_proj, "col")],
                name="ple")
    return h3, new_gla, new_conv


def kernel(x_prompt, x_sample, p_prompt, p_sample, state_gla, state_conv, ln_mix, w_in, w_gate_lr, b_gate, gla_norm, conv_w, conv_b, conv_ln_g, conv_ln_b, w_out, ln_moe, w_router, b_router, w_gu, b_gu, w_dn, b_dn, ln_ple, w_ple_gate, w_ple_proj, ln_final):
    bp, lp, d = x_prompt.shape
    bs, ls, _ = x_sample.shape
    depth = state_gla.shape[0]
    tp, ts = bp * lp, bs * ls
    groups = [(0, bp, lp), (tp, bs, ls)]
    h = jnp.concatenate([x_prompt.reshape(tp, d), x_sample.reshape(ts, d)], axis=0)
    zero_gla = jnp.zeros((bp,) + state_gla.shape[2:], state_gla.dtype)
    zero_conv = jnp.zeros((bp,) + state_conv.shape[2:], state_conv.dtype)
    gla_p, gla_s, conv_p, conv_s = [], [], [], []
    for i in range(depth):
        p = jnp.concatenate([p_prompt[i].reshape(tp, -1), p_sample[i].reshape(ts, -1)], axis=0)
        h, new_gla, new_conv = _layer(
            h, p, groups, [zero_gla, state_gla[i]], [zero_conv, state_conv[i]], ln_mix[i], w_in[i],
            w_gate_lr[i], b_gate[i], gla_norm[i], conv_w[i], conv_b[i], conv_ln_g[i], conv_ln_b[i],
            w_out[i], ln_moe[i], w_router[i], b_router[i], w_gu[i], b_gu[i], w_dn[i], b_dn[i],
            ln_ple[i], w_ple_gate[i], w_ple_proj[i])
        gla_p.append(new_gla[0])
        gla_s.append(new_gla[1])
        conv_p.append(new_conv[0])
        conv_s.append(new_conv[1])
    y_prompt = _rmsnorm(h, ln_final, F32, row0=0, rows=tp).reshape(bp, lp, d)
    y_sample = _rmsnorm(h, ln_final, F32, row0=tp, rows=ts).reshape(bs, ls, d)
    return (y_prompt, y_sample, jnp.stack(gla_p, 0), jnp.stack(conv_p, 0),
            jnp.stack(gla_s, 0), jnp.stack(conv_s, 0))
```
